```python
import math
import jax, jax.numpy as jnp
from jax import lax
import numpy as np

D_MODEL = 4096
BATCH = 2
SEQ = 4096
DEPTH = 4

PLE_DIM = 256
BLOCK = 128
EPS = 1e-6
FOX_HEADS = 16
FOX_HEAD_DIM = 128
FOX_WIDTH = FOX_HEADS * FOX_HEAD_DIM
SWA_Q_HEADS = 32
SWA_KV_HEADS = 4
SWA_HEAD_DIM = 64
SWA_WIDTH = SWA_Q_HEADS * SWA_HEAD_DIM
SWA_KV_WIDTH = SWA_KV_HEADS * SWA_HEAD_DIM
SWA_WINDOW = 128
DSA_HEADS = 32
DSA_QK_DIM = 128
DSA_V_DIM = 128
DSA_WIDTH = DSA_HEADS * DSA_V_DIM
DSA_Q_LATENT = 1024
DSA_KV_LATENT = 512
IDX_HEADS = 32
IDX_DIM = 64
TOPK_MAX = 256
T5_BUCKETS = 32
T5_MAX_DIST = 128
T5_COLS = SWA_Q_HEADS + DSA_HEADS

EVEN_SIZES = (FOX_WIDTH, FOX_WIDTH, FOX_WIDTH, FOX_HEADS, FOX_WIDTH,
              SWA_WIDTH, SWA_KV_WIDTH, SWA_KV_WIDTH, SWA_WIDTH)
EVEN_IN = 4 * FOX_WIDTH + FOX_HEADS + 2 * SWA_WIDTH + 2 * SWA_KV_WIDTH
EVEN_OUT = FOX_WIDTH + SWA_WIDTH
ODD_SIZES = (DSA_Q_LATENT, DSA_KV_LATENT, IDX_DIM, IDX_HEADS, DSA_WIDTH)
ODD_IN = DSA_Q_LATENT + DSA_KV_LATENT + IDX_DIM + IDX_HEADS + DSA_WIDTH
N_EVEN = (DEPTH + 1) // 2
N_ODD = DEPTH // 2

kernel_name = 'hybrid_fox_swa_dsa_block'

F32 = jnp.float32


def _split(t, sizes):
    offs = [int(v) for v in np.cumsum(sizes)[:-1]]
    return jnp.split(t, offs, axis=-1)


def rms_norm(x, g):
    xf = x.astype(F32)
    y = xf * lax.rsqrt(jnp.mean(xf * xf, axis=-1, keepdims=True) + EPS)
    return (y * g.astype(F32)).astype(x.dtype)


def layer_norm(x, g, b):
    xf = x.astype(F32)
    mu = jnp.mean(xf, axis=-1, keepdims=True)
    xc = xf - mu
    var = jnp.mean(xc * xc, axis=-1, keepdims=True)
    return (xc * lax.rsqrt(var + EPS) * g.astype(F32) + b.astype(F32)).astype(x.dtype)


def t5_bucket(rel):
    n = jnp.maximum(rel, 0)
    max_exact = T5_BUCKETS // 2
    nf = jnp.maximum(n, 1).astype(F32)
    large = max_exact + (jnp.log(nf / max_exact) / math.log(T5_MAX_DIST / max_exact)
                         * (T5_BUCKETS - max_exact)).astype(jnp.int32)
    large = jnp.minimum(large, T5_BUCKETS - 1)
    return jnp.where(n < max_exact, n, large)


def fox_attention(q, k, v, log_f):
    B, S, H, Dh = q.shape
    nb = S // BLOCK
    scale = Dh ** -0.5
    cum = jnp.cumsum(log_f, axis=1)
    cum_t = cum.transpose(0, 2, 1)
    kpos = jnp.arange(S)
    qb = q.reshape(B, nb, BLOCK, H, Dh).swapaxes(0, 1)
    cb = cum.reshape(B, nb, BLOCK, H).swapaxes(0, 1)

    def one_block(args):
        blk, q_blk, c_blk = args
        s = jnp.einsum('bqhd,bkhd->bhqk', q_blk, k, preferred_element_type=F32) * scale
        s = s + (c_blk.transpose(0, 2, 1)[:, :, :, None] - cum_t[:, :, None, :])
        qpos = blk * BLOCK + jnp.arange(BLOCK)
        causal = kpos[None, :] <= qpos[:, None]
        s = jnp.where(causal[None, None], s, -jnp.inf)
        pr = jax.nn.softmax(s, axis=-1)
        return jnp.einsum('bhqk,bkhd->bqhd', pr.astype(v.dtype), v)

    out = lax.map(one_block, (jnp.arange(nb), qb, cb))
    return out.swapaxes(0, 1).reshape(B, S, H, Dh)


def swa_sink_attention(q, k, v, sinks, band_bias):
    B, S, Hq, Dh = q.shape
    Hkv = k.shape[2]
    G = Hq // Hkv
    nb = S // BLOCK
    qb = q.reshape(B, nb, BLOCK, Hkv, G, Dh)

    def band(t):
        tb = t.reshape(B, nb, BLOCK, Hkv, Dh)
        prev = jnp.pad(tb, ((0, 0), (1, 0), (0, 0), (0, 0), (0, 0)))[:, :-1]
        return jnp.concatenate([prev, tb], axis=2)

    kb, vb = band(k), band(v)
    s = jnp.einsum('bnqhgd,bnkhd->bnhgqk', qb, kb, preferred_element_type=F32) * (Dh ** -0.5)
    s = s + band_bias.reshape(Hkv, G, BLOCK, 2 * BLOCK)
    qi = jnp.arange(BLOCK)[:, None]
    kj = jnp.arange(2 * BLOCK)[None, :]
    rel = qi + BLOCK - kj
    in_window = (rel >= 0) & (rel < SWA_WINDOW)
    has_prev = (jnp.arange(nb)[:, None, None] > 0) | (kj[None] >= BLOCK)
    valid = in_window[None] & has_prev
    s = jnp.where(valid[None, :, None, None], s, -jnp.inf)
    sink_col = jnp.broadcast_to(sinks.astype(F32).reshape(1, 1, Hkv, G, 1, 1), s.shape[:-1] + (1,))
    pr = jax.nn.softmax(jnp.concatenate([s, sink_col], axis=-1), axis=-1)[..., :-1]
    out = jnp.einsum('bnhgqk,bnkhd->bnqhgd', pr.astype(v.dtype), vb)
    return out.reshape(B, S, Hq, Dh)


def dsa_attention(q, c_kv, q_idx, k_idx, w_idx, w_uk, w_uv, t5_c, topk):
    B, S, H, _ = q.shape
    nb = S // BLOCK
    scale = DSA_QK_DIM ** -0.5
    kpos = jnp.arange(S)

    def to_blocks(t):
        return t.reshape((B, nb, BLOCK) + t.shape[2:]).swapaxes(0, 1)

    def one_block(args):
        blk, q_b, qi_b, wi_b = args
        qpos = blk * BLOCK + jnp.arange(BLOCK)
        causal = kpos[None, :] <= qpos[:, None]
        dots = jnp.einsum('bqhd,bkd->bqkh', qi_b, k_idx, preferred_element_type=F32)
        score = jnp.einsum('bqkh,bqh->bqk', jax.nn.relu(dots), wi_b.astype(F32))
        score = jnp.where(causal[None], score, -jnp.inf)
        _, idx = lax.top_k(score, topk)
        lat = jax.vmap(lambda c, i: c[i])(c_kv, idx)
        rel = qpos[None, :, None] - idx
        valid = rel >= 0
        bias = t5_c[t5_bucket(rel)]
        q_abs = jnp.einsum('bqhd,lhd->bqhl', q_b, w_uk)
        s = jnp.einsum('bqhl,bqkl->bqhk', q_abs, lat, preferred_element_type=F32) * scale
        s = s + bias.transpose(0, 1, 3, 2)
        s = jnp.where(valid[:, :, None, :], s, -jnp.inf)
        pr = jax.nn.softmax(s, axis=-1)
        o_lat = jnp.einsum('bqhk,bqkl->bqhl', pr.astype(lat.dtype), lat)
        return jnp.einsum('bqhl,lhd->bqhd', o_lat, w_uv)

    out = lax.map(one_block, (jnp.arange(nb), to_blocks(q), to_blocks(q_idx), to_blocks(w_idx)))
    return out.swapaxes(0, 1).reshape(B, S, H, DSA_V_DIM)


def even_mixer(hn, w_in, b_f, sinks, w_out, band_bias):
    B, S, _ = hn.shape
    proj = hn @ w_in
    q_a, k_a, v_a, f_a, g_a, q_b, k_b, v_b, g_b = _split(proj, EVEN_SIZES)
    log_f = jax.nn.log_sigmoid(f_a.astype(F32) + b_f.astype(F32))
    o_a = fox_attention(q_a.reshape(B, S, FOX_HEADS, FOX_HEAD_DIM),
                        k_a.reshape(B, S, FOX_HEADS, FOX_HEAD_DIM),
                        v_a.reshape(B, S, FOX_HEADS, FOX_HEAD_DIM), log_f)
    o_b = swa_sink_attention(q_b.reshape(B, S, SWA_Q_HEADS, SWA_HEAD_DIM),
                             k_b.reshape(B, S, SWA_KV_HEADS, SWA_HEAD_DIM),
                             v_b.reshape(B, S, SWA_KV_HEADS, SWA_HEAD_DIM), sinks, band_bias)
    y = jnp.concatenate([o_a.reshape(B, S, FOX_WIDTH) * jax.nn.silu(g_a),
                         o_b.reshape(B, S, SWA_WIDTH) * jax.nn.silu(g_b)], axis=-1)
    return y @ w_out


def odd_mixer(hn, w_in, q_norm_g, kv_norm_g, w_uq, w_uq_idx, ln_g, ln_b, w_uk, w_uv, w_out, t5_c, topk):
    B, S, _ = hn.shape
    proj = hn @ w_in
    c_q, c_kv, k_idx, w_idx, g_c = _split(proj, ODD_SIZES)
    c_q = rms_norm(c_q, q_norm_g)
    c_kv = rms_norm(c_kv, kv_norm_g)
    q = (c_q @ w_uq).reshape(B, S, DSA_HEADS, DSA_QK_DIM)
    q_idx = (c_q @ w_uq_idx).reshape(B, S, IDX_HEADS, IDX_DIM)
    k_idx = layer_norm(k_idx, ln_g, ln_b)
    w_idx = w_idx * (IDX_HEADS ** -0.5 * IDX_DIM ** -0.5)
    o = dsa_attention(q, c_kv, q_idx, k_idx, w_idx, w_uk, w_uv, t5_c, topk)
    y = o.reshape(B, S, DSA_WIDTH) * jax.nn.silu(g_c)
    return y @ w_out


def setup_inputs(seed: int = 0) -> dict:
    key = jax.random.key(seed)
    ks = jax.random.split(key, 24)
    n = jax.random.normal
    D = D_MODEL
    return {
        'x': n(ks[0], (BATCH, SEQ, D), F32),
        'p': n(ks[1], (DEPTH, BATCH, SEQ, PLE_DIM), F32),
        't5_table': 0.5 * n(ks[2], (T5_BUCKETS, T5_COLS), F32),
        'norm_g': 1.0 + 0.02 * n(ks[3], (DEPTH, D), F32),
        'even_w_in': n(ks[4], (N_EVEN, D, EVEN_IN), F32) * D ** -0.5,
        'even_b_f': 2.0 + 0.5 * n(ks[5], (N_EVEN, FOX_HEADS), F32),
        'even_sinks': 0.5 * n(ks[6], (N_EVEN, SWA_Q_HEADS), F32),
        'even_w_out': n(ks[7], (N_EVEN, EVEN_OUT, D), F32) * EVEN_OUT ** -0.5,
        'odd_w_in': n(ks[8], (N_ODD, D, ODD_IN), F32) * D ** -0.5,
        'odd_q_norm_g': 1.0 + 0.02 * n(ks[9], (N_ODD, DSA_Q_LATENT), F32),
        'odd_kv_norm_g': 1.0 + 0.02 * n(ks[10], (N_ODD, DSA_KV_LATENT), F32),
        'odd_w_uq': n(ks[11], (N_ODD, DSA_Q_LATENT, DSA_HEADS * DSA_QK_DIM), F32) * DSA_Q_LATENT ** -0.5,
        'odd_w_uq_idx': n(ks[12], (N_ODD, DSA_Q_LATENT, IDX_HEADS * IDX_DIM), F32) * DSA_Q_LATENT ** -0.5,
        'odd_idx_ln_g': 1.0 + 0.02 * n(ks[13], (N_ODD, IDX_DIM), F32),
        'odd_idx_ln_b': 0.02 * n(ks[14], (N_ODD, IDX_DIM), F32),
        'odd_w_uk': n(ks[15], (N_ODD, DSA_KV_LATENT, DSA_HEADS, DSA_QK_DIM), F32) * DSA_KV_LATENT ** -0.5,
        'odd_w_uv': n(ks[16], (N_ODD, DSA_KV_LATENT, DSA_HEADS, DSA_V_DIM), F32) * DSA_KV_LATENT ** -0.5,
        'odd_w_out': n(ks[17], (N_ODD, DSA_WIDTH, D), F32) * DSA_WIDTH ** -0.5,
        'ple_w_proj': n(ks[18], (DEPTH, PLE_DIM, D), F32) * PLE_DIM ** -0.5,
        'ple_norm_g': 1.0 + 0.02 * n(ks[19], (DEPTH, D), F32),
        'ple_w_gate': n(ks[20], (DEPTH, D, D), F32) * D ** -0.5,
        'final_g': 1.0 + 0.02 * n(ks[21], (D,), F32),
    }


def reference(x, p, t5_table, norm_g, even_w_in, even_b_f, even_sinks, even_w_out,
              odd_w_in, odd_q_norm_g, odd_kv_norm_g, odd_w_uq, odd_w_uq_idx, odd_idx_ln_g,
              odd_idx_ln_b, odd_w_uk, odd_w_uv, odd_w_out, ple_w_proj, ple_norm_g,
              ple_w_gate, final_g):
    S = x.shape[1]
    topk = min(TOPK_MAX, S // 4)
    rel_band = jnp.arange(BLOCK)[:, None] + BLOCK - jnp.arange(2 * BLOCK)[None, :]
    band_bias = t5_table[t5_bucket(rel_band)][..., :SWA_Q_HEADS].transpose(2, 0, 1)
    t5_c = t5_table[:, SWA_Q_HEADS:]
    h = x
    for i in range(DEPTH):
        j = i // 2
        hn = rms_norm(h, norm_g[i])
        if i % 2 == 0:
            y = even_mixer(hn, even_w_in[j], even_b_f[j], even_sinks[j], even_w_out[j], band_bias)
        else:
            y = odd_mixer(hn, odd_w_in[j], odd_q_norm_g[j], odd_kv_norm_g[j], odd_w_uq[j],
                          odd_w_uq_idx[j], odd_idx_ln_g[j], odd_idx_ln_b[j], odd_w_uk[j],
                          odd_w_uv[j], odd_w_out[j], t5_c, topk)
        h = h + y
        gate = jax.nn.sigmoid(rms_norm(h, ple_norm_g[i]) @ ple_w_gate[i])
        h = h + gate * (p[i] @ ple_w_proj[i])
    return rms_norm(h, final_g)
```

```python
import functools
import math

import jax
import jax.numpy as jnp
from jax import lax
from jax.experimental import pallas as pl
from jax.experimental.pallas import tpu as pltpu

F32 = jnp.float32
BF16 = jnp.bfloat16
I32 = jnp.int32

EPS = 1e-6
BLOCK = 128
PLE_DIM = 256
FOX_HEADS = 16
FOX_HEAD_DIM = 128
FOX_WIDTH = FOX_HEADS * FOX_HEAD_DIM
SWA_Q_HEADS = 32
SWA_KV_HEADS = 4
SWA_GROUP = SWA_Q_HEADS // SWA_KV_HEADS
SWA_HEAD_DIM = 64
SWA_WIDTH = SWA_Q_HEADS * SWA_HEAD_DIM
SWA_KV_WIDTH = SWA_KV_HEADS * SWA_HEAD_DIM
DSA_HEADS = 32
DSA_QK_DIM = 128
DSA_V_DIM = 128
DSA_WIDTH = DSA_HEADS * DSA_V_DIM
DSA_Q_LATENT = 1024
DSA_KV_LATENT = 512
IDX_HEADS = 32
IDX_DIM = 64
IDX_WIDTH = IDX_HEADS * IDX_DIM
TOPK_MAX = 256
T5_BUCKETS = 32
T5_MAX_DIST = 128

LANES = 128
V7X_VMEM_BYTES = 64 * 1024 * 1024
NEG = -1e30
INT_MIN = -2147483648


def _cparams(n_axes, vmem_bytes):
    limit = min(int(vmem_bytes * 1.25) + (4 << 20), V7X_VMEM_BYTES - (8 << 20))
    return pltpu.CompilerParams(dimension_semantics=("arbitrary",) * n_axes,
                                vmem_limit_bytes=limit)


def _pick_tile(n, cap):
    if n <= cap:
        return n
    best = None
    for d in range(LANES, cap + 1, LANES):
        if n % d == 0:
            best = d
    assert best is not None, (n, cap)
    return best


def _nbytes(shape, dtype):
    n = 1
    for s in shape:
        n *= s
    return n * jnp.dtype(dtype).itemsize


def _rmsnorm_body(x_ref, g_ref, o_ref):
    x = x_ref[...]
    ms = jnp.mean(x * x, axis=-1, keepdims=True)
    o_ref[...] = (x * lax.rsqrt(ms + EPS) * g_ref[...]).astype(o_ref.dtype)


def _rmsnorm(x, g, out_dtype):
    M, D = x.shape
    tm = _pick_tile(M, 256)
    vm = 2 * (_nbytes((tm, D), F32) + _nbytes((tm, D), out_dtype)) + 3 * _nbytes((tm, D), F32)
    return pl.pallas_call(
        _rmsnorm_body,
        grid=(M // tm,),
        in_specs=[pl.BlockSpec((tm, D), lambda i: (i, 0)),
                  pl.BlockSpec((1, D), lambda i: (0, 0))],
        out_specs=pl.BlockSpec((tm, D), lambda i: (i, 0)),
        out_shape=jax.ShapeDtypeStruct((M, D), out_dtype),
        compiler_params=_cparams(1, vm),
        name="rmsnorm",
    )(x, g.reshape(1, D).astype(F32))


def _matmul_body(*refs, n_pairs, has_res):
    o_ref = refs[-1]
    acc = None
    for x_ref, w_ref in zip(refs[:n_pairs], refs[n_pairs:2 * n_pairs]):
        d = jnp.dot(x_ref[...], w_ref[...], preferred_element_type=F32)
        acc = d if acc is None else acc + d
    if has_res:
        acc = acc + refs[2 * n_pairs][...]
    o_ref[...] = acc.astype(o_ref.dtype)


def _matmul(pairs, out_dtype, *, residual=None, tm_cap=512, tn_cap=1536, name="matmul"):
    M = pairs[0][0].shape[0]
    N = pairs[0][1].shape[1]
    tm = _pick_tile(M, tm_cap)
    tn = _pick_tile(N, tn_cap)
    in_specs, args, vm = [], [], 0
    for x, _ in pairs:
        K = x.shape[1]
        in_specs.append(pl.BlockSpec((tm, K), lambda j, i: (i, 0)))
        args.append(x)
        vm += 2 * _nbytes((tm, K), x.dtype)
    for _, w in pairs:
        K = w.shape[0]
        in_specs.append(pl.BlockSpec((K, tn), lambda j, i: (0, j)))
        args.append(w)
        vm += 2 * _nbytes((K, tn), w.dtype)
    if residual is not None:
        in_specs.append(pl.BlockSpec((tm, tn), lambda j, i: (i, j)))
        args.append(residual)
        vm += 2 * _nbytes((tm, tn), F32)
    vm += 2 * _nbytes((tm, tn), out_dtype) + 2 * _nbytes((tm, tn), F32)
    return pl.pallas_call(
        functools.partial(_matmul_body, n_pairs=len(pairs), has_res=residual is not None),
        grid=(N // tn, M // tm),
        in_specs=in_specs,
        out_specs=pl.BlockSpec((tm, tn), lambda j, i: (i, j)),
        out_shape=jax.ShapeDtypeStruct((M, N), out_dtype),
        compiler_params=_cparams(2, vm),
        name=name,
    )(*args)


def _ple_body(hn_ref, wg_ref, p_ref, wp_ref, h_ref, o_ref):
    z = jnp.dot(hn_ref[...], wg_ref[...], preferred_element_type=F32)
    gate = 1.0 / (1.0 + jnp.exp(-z))
    pe = jnp.dot(p_ref[...].astype(BF16), wp_ref[...], preferred_element_type=F32)
    o_ref[...] = h_ref[...] + gate * pe


def _ple(hn, wg, p_all, layer, wp, h):
    M, D = h.shape
    P = p_all.shape[-1]
    tm = _pick_tile(M, 512)
    tn = _pick_tile(D, 1024)
    vm = (2 * (_nbytes((tm, D), BF16) + _nbytes((D, tn), BF16) + _nbytes((tm, P), F32)
               + _nbytes((P, tn), BF16) + 2 * _nbytes((tm, tn), F32))
          + 3 * _nbytes((tm, tn), F32))
    return pl.pallas_call(
        _ple_body,
        grid=(D // tn, M // tm),
        in_specs=[pl.BlockSpec((tm, D), lambda j, i: (i, 0)),
                  pl.BlockSpec((D, tn), lambda j, i: (0, j)),
                  pl.BlockSpec((None, tm, P), lambda j, i: (layer, i, 0)),
                  pl.BlockSpec((P, tn), lambda j, i: (0, j)),
                  pl.BlockSpec((tm, tn), lambda j, i: (i, j))],
        out_specs=pl.BlockSpec((tm, tn), lambda j, i: (i, j)),
        out_shape=jax.ShapeDtypeStruct((M, D), F32),
        compiler_params=_cparams(2, vm),
        name="ple_gate",
    )(hn, wg, p_all, wp, h)


def _cum_body(f_ref, b_ref, o_ref, scr, *, S):
    x = f_ref[...] + b_ref[...]
    neg_log_f = jnp.maximum(-x, 0.0) + jnp.log1p(jnp.exp(-jnp.abs(x)))
    r = lax.broadcasted_iota(I32, (BLOCK, BLOCK), 0)
    c = lax.broadcasted_iota(I32, (BLOCK, BLOCK), 1)
    tri = (c <= r).astype(BF16)
    run = jnp.zeros((1, LANES), F32)
    for i in range(S // BLOCK):
        blk = neg_log_f[i * BLOCK:(i + 1) * BLOCK, :]
        hi = blk.astype(BF16)
        r1 = blk - hi.astype(F32)
        mid = r1.astype(BF16)
        lo = (r1 - mid.astype(F32)).astype(BF16)
        cs = (jnp.dot(tri, hi, preferred_element_type=F32)
              + jnp.dot(tri, mid, preferred_element_type=F32)
              + jnp.dot(tri, lo, preferred_element_type=F32)) + run
        scr[i * BLOCK:(i + 1) * BLOCK, :] = cs
        run = cs[BLOCK - 1:BLOCK, :]
    o_ref[0] = scr[...].T[:FOX_HEADS, :]


def _fox_cum(f, b_f, B, S):
    b_pad = jnp.zeros((1, LANES), F32).at[0, :FOX_HEADS].set(b_f.astype(F32))
    vm = 6 * _nbytes((S, LANES), F32)
    return pl.pallas_call(
        functools.partial(_cum_body, S=S),
        grid=(B,),
        in_specs=[pl.BlockSpec((S, LANES), lambda b: (b, 0)),
                  pl.BlockSpec((1, LANES), lambda b: (0, 0))],
        out_specs=pl.BlockSpec((1, FOX_HEADS, S), lambda b: (b, 0, 0)),
        out_shape=jax.ShapeDtypeStruct((B, FOX_HEADS, S), F32),
        scratch_shapes=[pltpu.VMEM((S, LANES), F32)],
        compiler_params=_cparams(1, vm),
        name="fox_decay_cumsum",
    )(f, b_pad)


def _softmax_step(s, v, carry):
    m, l, acc = carry
    m_new = jnp.maximum(m, jnp.max(s, axis=-1, keepdims=True))
    alpha = jnp.exp(m - m_new)
    p = jnp.exp(s - m_new)
    l = alpha * l + jnp.sum(p, axis=-1, keepdims=True)
    acc = alpha * acc + jnp.dot(p.astype(BF16), v, preferred_element_type=F32)
    return m_new, l, acc


def _flash_body(*refs, T, hb, fox):
    if fox:
        q_ref, k_ref, v_ref, g_ref, c_ref, o_ref = refs
    else:
        q_ref, k_ref, v_ref, g_ref, mb_ref, d_ref, o_ref, bias_scr = refs
    qi = pl.program_id(1)
    hd = LANES

    if not fox:
        first = (pl.program_id(0) == 0) & (qi == 0) & (pl.program_id(2) == 0)

        @pl.when(first)
        def _():
            bias_scr[...] = jnp.zeros(bias_scr.shape, F32)

        for hh in range(hb):
            for a in range(T // BLOCK):
                lo = T + (a - 1) * BLOCK
                bias_scr[hh, a * BLOCK:(a + 1) * BLOCK, lo:lo + 2 * BLOCK] = d_ref[hh]

    for hh in range(hb):
        cols = slice(hh * hd, (hh + 1) * hd)
        q = q_ref[:, cols]

        def step(j, carry, kind, cols=cols, q=q, hh=hh):
            off = pl.multiple_of(j * T, T)
            k = k_ref[pl.ds(off, T), cols]
            v = v_ref[pl.ds(off, T), cols]
            s = lax.dot_general(q, k, (((1,), (1,)), ((), ())), preferred_element_type=F32)
            if fox:
                s = s + c_ref[0, :, pl.ds(off, T)]
                if kind == "diag":
                    row = lax.broadcasted_iota(I32, (T, T), 0)
                    col = lax.broadcasted_iota(I32, (T, T), 1)
                    s = jnp.where(row >= col, s, NEG)
            else:
                s = s + mb_ref[:, pl.ds(off, T)].astype(F32)
                if kind == "prev":
                    s = s + bias_scr[hh, :, 0:T]
                elif kind == "diag":
                    s = s + bias_scr[hh, :, T:2 * T]
            return _softmax_step(s, v, carry)

        carry = (jnp.full((T, 1), NEG, F32), jnp.zeros((T, 1), F32), jnp.zeros((T, hd), F32))
        if fox:
            carry = lax.fori_loop(0, qi, functools.partial(step, kind="plain"), carry)
        else:
            n_plain = jnp.maximum(qi - 1, 0)
            carry = lax.fori_loop(0, n_plain, functools.partial(step, kind="plain"), carry)
            carry = lax.fori_loop(n_plain, qi, functools.partial(step, kind="prev"), carry)
        _, l, acc = step(qi, carry, "diag")
        g = g_ref[:, cols].astype(F32)
        o_ref[:, cols] = (acc / l * (g / (1.0 + jnp.exp(-g)))).astype(o_ref.dtype)


def _flash(q_arr, k_arr, v_arr, g_arr, *, B, S, H, q_blk, k_blk, v_blk, g_blk, hb, T,
           decay=None, mask_bias=None, band=None):
    fox = decay is not None
    M = B * S
    nq = S // T
    w = hb * LANES
    qb, kb, vb, gb = q_blk // hb, k_blk // hb, v_blk // hb, g_blk // hb
    in_specs = [pl.BlockSpec((T, w), lambda b, i, h: (b * nq + i, qb + h)),
                pl.BlockSpec((S, w), lambda b, i, h: (b, kb + h)),
                pl.BlockSpec((S, w), lambda b, i, h: (b, vb + h)),
                pl.BlockSpec((T, w), lambda b, i, h: (b * nq + i, gb + h))]
    args = [q_arr, k_arr, v_arr, g_arr]
    vm = 2 * (3 * _nbytes((T, w), BF16) + 2 * _nbytes((S, w), BF16)) + 8 * _nbytes((T, T), F32)
    scratch = []
    if fox:
        in_specs.append(pl.BlockSpec((1, 1, S), lambda b, i, h: (b * H + h, 0, 0)))
        args.append(decay.reshape(B * H, 1, S))
        vm += 2 * _nbytes((8, S), F32)
    else:
        in_specs.append(pl.BlockSpec((T, S), lambda b, i, h: (b * nq + i, 0)))
        in_specs.append(pl.BlockSpec((hb, BLOCK, 2 * BLOCK), lambda b, i, h: (h, 0, 0)))
        args += [mask_bias, band]
        scratch = [pltpu.VMEM((hb, T, 2 * T), F32)]
        vm += 2 * _nbytes((T, S), BF16) + _nbytes((hb, T, 2 * T), F32) + 2 * _nbytes((hb, BLOCK, 2 * BLOCK), F32)
    return pl.pallas_call(
        functools.partial(_flash_body, T=T, hb=hb, fox=fox),
        grid=(B, nq, H // hb),
        in_specs=in_specs,
        out_specs=pl.BlockSpec((T, w), lambda b, i, h: (b * nq + i, h)),
        out_shape=jax.ShapeDtypeStruct((M, H * LANES), BF16),
        scratch_shapes=scratch,
        compiler_params=_cparams(3, vm),
        name="fox_attention" if fox else "dsa_attention",
    )(*args)


def _swa_body(sink_ref, q_ref, kp_ref, kc_ref, vp_ref, vc_ref, g_ref, bias_ref, o_ref):
    n = pl.program_id(1)
    G, dh = SWA_GROUP, SWA_HEAD_DIM
    rows = G * BLOCK
    r = lax.broadcasted_iota(I32, (rows, 2 * BLOCK), 0) & (BLOCK - 1)
    c = lax.broadcasted_iota(I32, (rows, 2 * BLOCK), 1)
    d = c - r
    first_col = jnp.where(n > 0, 0, BLOCK)
    valid = (d >= 1) & (d <= BLOCK) & (c >= first_col)
    head_of_row = lax.broadcasted_iota(I32, (rows, 1), 0) >> 7
    outs = []
    for kvh in range(SWA_KV_HEADS):
        kcols = slice(kvh * dh, (kvh + 1) * dh)
        k = jnp.concatenate([kp_ref[:, kcols], kc_ref[:, kcols]], axis=0)
        v = jnp.concatenate([vp_ref[:, kcols], vc_ref[:, kcols]], axis=0)
        qs = jnp.concatenate(
            [q_ref[:, (kvh * G + i) * dh:(kvh * G + i + 1) * dh] for i in range(G)], axis=0)
        s = lax.dot_general(qs, k, (((1,), (1,)), ((), ())), preferred_element_type=F32)
        s = s + bias_ref[kvh * G:(kvh + 1) * G].reshape(rows, 2 * BLOCK)
        s = jnp.where(valid, s, NEG)
        sink = jnp.zeros((rows, 1), F32)
        for i in range(G):
            sink = jnp.where(head_of_row == i, sink_ref[kvh * G + i], sink)
        m = jnp.maximum(jnp.max(s, axis=-1, keepdims=True), sink)
        p = jnp.exp(s - m)
        den = jnp.sum(p, axis=-1, keepdims=True) + jnp.exp(sink - m)
        o = jnp.dot(p.astype(BF16), v, preferred_element_type=F32) / den
        outs += [o[i * BLOCK:(i + 1) * BLOCK, :] for i in range(G)]
    o_all = jnp.concatenate(outs, axis=1)
    g = g_ref[...].astype(F32)
    o_ref[...] = (o_all * (g / (1.0 + jnp.exp(-g)))).astype(o_ref.dtype)


def _swa(proj, sinks, band, *, B, S, q_blk, g_blk, k_blk, v_blk):
    M = B * S
    nb = S // BLOCK
    W, KW = SWA_WIDTH, SWA_KV_WIDTH

    def prev(b, n):
        return b * nb + jnp.maximum(n - 1, 0)

    vm = (2 * (3 * _nbytes((BLOCK, W), BF16) + 4 * _nbytes((BLOCK, KW), BF16))
          + 2 * _nbytes(band.shape, F32) + 16 * _nbytes((SWA_GROUP * BLOCK, 2 * BLOCK), F32))
    return pl.pallas_call(
        _swa_body,
        grid=(B, nb),
        in_specs=[pl.BlockSpec(memory_space=pltpu.SMEM),
                  pl.BlockSpec((BLOCK, W), lambda b, n: (b * nb + n, q_blk)),
                  pl.BlockSpec((BLOCK, KW), lambda b, n: (prev(b, n), k_blk)),
                  pl.BlockSpec((BLOCK, KW), lambda b, n: (b * nb + n, k_blk)),
                  pl.BlockSpec((BLOCK, KW), lambda b, n: (prev(b, n), v_blk)),
                  pl.BlockSpec((BLOCK, KW), lambda b, n: (b * nb + n, v_blk)),
                  pl.BlockSpec((BLOCK, W), lambda b, n: (b * nb + n, g_blk)),
                  pl.BlockSpec(band.shape, lambda b, n: (0, 0, 0))],
        out_specs=pl.BlockSpec((BLOCK, W), lambda b, n: (b * nb + n, 0)),
        out_shape=jax.ShapeDtypeStruct((M, W), BF16),
        compiler_params=_cparams(2, vm),
        name="swa_sink_attention",
    )(sinks.astype(F32), proj, proj, proj, proj, proj, proj, band)


_ODD_SMALL = DSA_Q_LATENT + DSA_KV_LATENT + LANES


def _odd_post_body(x_ref, qg_ref, kvg_ref, lng_ref, lnb_ref, cq_ref, ckv_ref, kidx_ref, w_ref):
    cq = x_ref[:, 0:DSA_Q_LATENT]
    cq_ref[...] = (cq * lax.rsqrt(jnp.mean(cq * cq, axis=-1, keepdims=True) + EPS)
                   * qg_ref[...]).astype(cq_ref.dtype)
    ckv = x_ref[:, DSA_Q_LATENT:DSA_Q_LATENT + DSA_KV_LATENT]
    ckv_ref[...] = (ckv * lax.rsqrt(jnp.mean(ckv * ckv, axis=-1, keepdims=True) + EPS)
                    * kvg_ref[...]).astype(ckv_ref.dtype)
    tail = x_ref[:, DSA_Q_LATENT + DSA_KV_LATENT:_ODD_SMALL]
    ki = tail[:, 0:IDX_DIM]
    mu = jnp.mean(ki, axis=-1, keepdims=True)
    xc = ki - mu
    var = jnp.mean(xc * xc, axis=-1, keepdims=True)
    kidx_ref[...] = (xc * lax.rsqrt(var + EPS) * lng_ref[...] + lnb_ref[...]).astype(kidx_ref.dtype)
    w_ref[...] = tail[:, IDX_DIM:IDX_DIM + IDX_HEADS] * (IDX_HEADS ** -0.5 * IDX_DIM ** -0.5)


def _odd_post(small, q_norm_g, kv_norm_g, ln_g, ln_b):
    M = small.shape[0]
    tm = _pick_tile(M, 512)
    vm = 8 * _nbytes((tm, _ODD_SMALL), F32)
    row = lambda i: (i, 0)
    fixed = lambda i: (0, 0)
    return pl.pallas_call(
        _odd_post_body,
        grid=(M // tm,),
        in_specs=[pl.BlockSpec((tm, _ODD_SMALL), row),
                  pl.BlockSpec((1, DSA_Q_LATENT), fixed),
                  pl.BlockSpec((1, DSA_KV_LATENT), fixed),
                  pl.BlockSpec((1, IDX_DIM), fixed),
                  pl.BlockSpec((1, IDX_DIM), fixed)],
        out_specs=[pl.BlockSpec((tm, DSA_Q_LATENT), row),
                   pl.BlockSpec((tm, DSA_KV_LATENT), row),
                   pl.BlockSpec((tm, IDX_DIM), row),
                   pl.BlockSpec((tm, IDX_HEADS), row)],
        out_shape=[jax.ShapeDtypeStruct((M, DSA_Q_LATENT), BF16),
                   jax.ShapeDtypeStruct((M, DSA_KV_LATENT), BF16),
                   jax.ShapeDtypeStruct((M, IDX_DIM), BF16),
                   jax.ShapeDtypeStruct((M, IDX_HEADS), F32)],
        compiler_params=_cparams(1, vm),
        name="dsa_latent_norms",
    )(small, q_norm_g.reshape(1, -1).astype(F32), kv_norm_g.reshape(1, -1).astype(F32),
      ln_g.reshape(1, -1).astype(F32), ln_b.reshape(1, -1).astype(F32))


def _indexer_body(q_ref, k_ref, w_ref, o_ref, key_scr, *, tq, tk, S, topk):
    qi = pl.program_id(1)
    nck = (qi * tq) // tk + 1
    nt = tk // LANES
    rowpos = qi * tq + lax.broadcasted_iota(I32, (tq, tk), 0)
    colpos0 = lax.broadcasted_iota(I32, (tq, tk), 1)

    def score_chunk(c, _):
        off = pl.multiple_of(c * tk, tk)
        kc = k_ref[pl.ds(off, tk), :]
        acc = jnp.zeros((tq, tk), F32)
        for h in range(IDX_HEADS):
            qh = q_ref[:, h * IDX_DIM:(h + 1) * IDX_DIM]
            d = lax.dot_general(qh, kc, (((1,), (1,)), ((), ())), preferred_element_type=F32)
            acc = acc + jnp.maximum(d, 0.0) * w_ref[:, h:h + 1]
        bits = pltpu.bitcast(acc, I32)
        key = bits ^ ((bits >> 31) & 0x7FFFFFFF)
        key = jnp.where(colpos0 + off <= rowpos, key, INT_MIN)
        key_scr[:, pl.ds(off, tk)] = key
        return 0

    lax.fori_loop(0, nck, score_chunk, 0)

    def count(pred):
        def body(c, acc):
            off = pl.multiple_of(c * tk, tk)
            kk = key_scr[:, pl.ds(off, tk)]
            for t in range(nt):
                acc = acc + jnp.where(pred(kk[:, t * LANES:(t + 1) * LANES]), 1, 0)
            return acc
        part = lax.fori_loop(0, nck, body, jnp.zeros((tq, LANES), I32))
        return jnp.sum(part, axis=-1, keepdims=True)

    def bisect(i, t_u):
        cand_u = t_u | lax.shift_left(jnp.int32(1), 31 - i)
        cand = cand_u ^ INT_MIN
        cnt = count(lambda kk: kk >= cand)
        return jnp.where(cnt >= topk, cand_u, t_u)

    t_u = lax.fori_loop(0, 32, bisect, jnp.zeros((tq, 1), I32))
    thr = t_u ^ INT_MIN
    has_thr = thr > INT_MIN
    thr_sel = jnp.maximum(thr, INT_MIN + 1)
    n_ge = count(lambda kk: kk >= thr)
    tie = jnp.max(jnp.where(has_thr & (n_ge > topk), 1, 0)) > 0

    def write_rest():
        def body(c, _):
            off = pl.multiple_of(c * tk, tk)
            o_ref[:, pl.ds(off, tk)] = jnp.full((tq, tk), NEG, o_ref.dtype)
            return 0
        lax.fori_loop(nck, S // tk, body, 0)

    @pl.when(jnp.logical_not(tie))
    def _():
        def body(c, _):
            off = pl.multiple_of(c * tk, tk)
            kk = key_scr[:, pl.ds(off, tk)]
            o_ref[:, pl.ds(off, tk)] = jnp.where(kk >= thr_sel, 0.0, NEG).astype(o_ref.dtype)
            return 0
        lax.fori_loop(0, nck, body, 0)

    @pl.when(tie)
    def _():
        n_gt = count(lambda kk: kk > thr)
        room = jnp.where(has_thr, topk - n_gt, 0).astype(F32)
        r = lax.broadcasted_iota(I32, (tk, tk), 0)
        cc = lax.broadcasted_iota(I32, (tk, tk), 1)
        upper = (r <= cc).astype(BF16)

        def body(c, run):
            off = pl.multiple_of(c * tk, tk)
            kk = key_scr[:, pl.ds(off, tk)]
            eq = kk == thr
            eq_f = jnp.where(eq, 1.0, 0.0)
            rank = jnp.dot(eq_f.astype(BF16), upper, preferred_element_type=F32) + run
            sel = (kk > thr) | (eq & (rank <= room))
            o_ref[:, pl.ds(off, tk)] = jnp.where(sel, 0.0, NEG).astype(o_ref.dtype)
            return run + jnp.sum(eq_f, axis=-1, keepdims=True)
        lax.fori_loop(0, nck, body, jnp.zeros((tq, 1), F32))

    write_rest()


def _indexer(qq, kidx, widx, *, B, S, topk, q_blk):
    M = B * S
    tq = BLOCK
    tk = _pick_tile(S, 512)
    nq = S // tq
    vm = (2 * (_nbytes((tq, IDX_WIDTH), BF16) + _nbytes((S, LANES), BF16) + _nbytes((tq, LANES), F32)
               + _nbytes((tq, S), BF16)) + _nbytes((tq, S), I32) + 16 * _nbytes((tq, tk), F32))
    return pl.pallas_call(
        functools.partial(_indexer_body, tq=tq, tk=tk, S=S, topk=topk),
        grid=(B, nq),
        in_specs=[pl.BlockSpec((tq, IDX_WIDTH), lambda b, i: (b * nq + i, q_blk)),
                  pl.BlockSpec((S, IDX_DIM), lambda b, i: (b, 0)),
                  pl.BlockSpec((tq, IDX_HEADS), lambda b, i: (b * nq + i, 0))],
        out_specs=pl.BlockSpec((tq, S), lambda b, i: (b * nq + i, 0)),
        out_shape=jax.ShapeDtypeStruct((M, S), BF16),
        scratch_shapes=[pltpu.VMEM((tq, S), I32)],
        compiler_params=_cparams(2, vm),
        name="dsa_indexer_topk",
    )(qq, kidx, widx)


def _t5_bucket(rel):
    n = jnp.maximum(rel, 0)
    max_exact = T5_BUCKETS // 2
    nf = jnp.maximum(n, 1).astype(F32)
    large = max_exact + (jnp.log(nf / max_exact) / math.log(T5_MAX_DIST / max_exact)
                         * (T5_BUCKETS - max_exact)).astype(I32)
    large = jnp.minimum(large, T5_BUCKETS - 1)
    return jnp.where(n < max_exact, n, large)


def _band_tables(t5_table):
    rel = jnp.arange(BLOCK)[:, None] + BLOCK - jnp.arange(2 * BLOCK)[None, :]
    band = t5_table.astype(F32)[_t5_bucket(rel)].transpose(2, 0, 1)
    swa = band[:SWA_Q_HEADS]
    far = t5_table.astype(F32)[T5_BUCKETS - 1, SWA_Q_HEADS:]
    dsa = band[SWA_Q_HEADS:] - far[:, None, None]
    return swa, dsa


def _flash_tile(S):
    return _pick_tile(S, 512)


def _even_layer(h, hn, w_in, b_f, sinks, w_out, band_swa, B, S):
    o = [0]
    for sz in (FOX_WIDTH, FOX_WIDTH, FOX_WIDTH, FOX_HEADS, FOX_WIDTH,
               SWA_WIDTH, SWA_KV_WIDTH, SWA_KV_WIDTH, SWA_WIDTH):
        o.append(o[-1] + sz)
    seg = lambda i: w_in[:, o[i]:o[i + 1]]
    w_main = jnp.concatenate(
        [seg(0) * (FOX_HEAD_DIM ** -0.5), seg(1), seg(2), seg(4),
         seg(5) * (SWA_HEAD_DIM ** -0.5), seg(8), seg(6), seg(7)], axis=1).astype(BF16)
    w_f = jnp.pad(seg(3), ((0, 0), (0, LANES - FOX_HEADS))).astype(BF16)
    proj = _matmul([(hn, w_main)], BF16, name="even_in_proj")
    f = _matmul([(hn, w_f)], F32, name="even_gate_proj")
    decay = _fox_cum(f, b_f, B, S)
    nh = FOX_HEADS
    ya = _flash(proj, proj, proj, proj, B=B, S=S, H=nh, q_blk=0, k_blk=nh, v_blk=2 * nh,
                g_blk=3 * nh, hb=1, T=_flash_tile(S), decay=decay)
    base = 4 * FOX_WIDTH
    yb = _swa(proj, sinks, band_swa, B=B, S=S,
              q_blk=base // SWA_WIDTH, g_blk=base // SWA_WIDTH + 1,
              k_blk=(base + 2 * SWA_WIDTH) // SWA_KV_WIDTH,
              v_blk=(base + 2 * SWA_WIDTH) // SWA_KV_WIDTH + 1)
    w_o = w_out.astype(BF16)
    return _matmul([(ya, w_o[:FOX_WIDTH]), (yb, w_o[FOX_WIDTH:])], F32, residual=h,
                   tn_cap=1024, name="even_out_proj")


def _odd_layer(h, hn, w_in, q_norm_g, kv_norm_g, w_uq, w_uq_idx, ln_g, ln_b, w_uk, w_uv, w_out,
               band_dsa, B, S, topk):
    n_small = DSA_Q_LATENT + DSA_KV_LATENT + IDX_DIM + IDX_HEADS
    w_small = jnp.pad(w_in[:, :n_small], ((0, 0), (0, _ODD_SMALL - n_small))).astype(BF16)
    w_g = w_in[:, n_small:].astype(BF16)
    small = _matmul([(hn, w_small)], F32, tm_cap=256, tn_cap=_ODD_SMALL, name="odd_latent_proj")
    gc = _matmul([(hn, w_g)], BF16, tn_cap=1024, name="odd_gate_proj")
    cq, ckv, kidx, widx = _odd_post(small, q_norm_g, kv_norm_g, ln_g, ln_b)
    w_q = jnp.concatenate([w_uq * (DSA_QK_DIM ** -0.5), w_uq_idx], axis=1).astype(BF16)
    qq = _matmul([(cq, w_q)], BF16, tn_cap=2048, name="odd_q_up")
    L = w_uk.shape[0]
    w_kv = jnp.concatenate([w_uk.reshape(L, DSA_WIDTH), w_uv.reshape(L, DSA_WIDTH)], axis=1).astype(BF16)
    kv = _matmul([(ckv, w_kv)], BF16, tn_cap=2048, name="odd_kv_up")
    mask_bias = _indexer(qq, kidx, widx, B=B, S=S, topk=topk, q_blk=DSA_WIDTH // IDX_WIDTH)
    nh = DSA_HEADS
    y = _flash(qq, kv, kv, gc, B=B, S=S, H=nh, q_blk=0, k_blk=0, v_blk=nh, g_blk=0, hb=2,
               T=_flash_tile(S), mask_bias=mask_bias, band=band_dsa)
    return _matmul([(y, w_out.astype(BF16))], F32, residual=h, tn_cap=1024, name="odd_out_proj")


def kernel(x, p, t5_table, norm_g, even_w_in, even_b_f, even_sinks, even_w_out, odd_w_in, odd_q_norm_g, odd_kv_norm_g, odd_w_uq, odd_w_uq_idx, odd_idx_ln_g, odd_idx_ln_b, odd_w_uk, odd_w_uv, odd_w_out, ple_w_proj, ple_norm_g, ple_w_gate, final_g):
    B, S, D = x.shape
    depth = norm_g.shape[0]
    M = B * S
    topk = min(TOPK_MAX, S // 4)
    band_swa, band_dsa = _band_tables(t5_table)
    h = x.reshape(M, D).astype(F32)
    p_all = p.reshape(depth, M, p.shape[-1]).astype(F32)
    for i in range(depth):
        j = i // 2
        hn = _rmsnorm(h, norm_g[i], BF16)
        if i % 2 == 0:
            h = _even_layer(h, hn, even_w_in[j], even_b_f[j], even_sinks[j], even_w_out[j],
                            band_swa, B, S)
        else:
            h = _odd_layer(h, hn, odd_w_in[j], odd_q_norm_g[j], odd_kv_norm_g[j], odd_w_uq[j],
                           odd_w_uq_idx[j], odd_idx_ln_g[j], odd_idx_ln_b[j], odd_w_uk[j],
                           odd_w_uv[j], odd_w_out[j], band_dsa, B, S, topk)
        hn = _rmsnorm(h, ple_norm_g[i], BF16)
        h = _ple(hn, ple_w_gate[i].astype(BF16), p_all, i, ple_w_proj[i].astype(BF16), h)
    return _rmsnorm(h, final_g, F32).reshape(B, S, D)
```

```python
import functools
import math

import jax
import jax.numpy as jnp
from jax import lax
from jax.experimental import pallas as pl
from jax.experimental.pallas import tpu as pltpu

F32 = jnp.float32
BF16 = jnp.bfloat16
I32 = jnp.int32

EPS = 1e-6
BLOCK = 128
PLE_DIM = 256
FOX_HEADS = 16
FOX_HEAD_DIM = 128
FOX_WIDTH = FOX_HEADS * FOX_HEAD_DIM
SWA_Q_HEADS = 32
SWA_KV_HEADS = 4
SWA_GROUP = SWA_Q_HEADS // SWA_KV_HEADS
SWA_HEAD_DIM = 64
SWA_WIDTH = SWA_Q_HEADS * SWA_HEAD_DIM
SWA_KV_WIDTH = SWA_KV_HEADS * SWA_HEAD_DIM
DSA_HEADS = 32
DSA_QK_DIM = 128
DSA_V_DIM = 128
DSA_WIDTH = DSA_HEADS * DSA_V_DIM
DSA_Q_LATENT = 1024
DSA_KV_LATENT = 512
IDX_HEADS = 32
IDX_DIM = 64
IDX_WIDTH = IDX_HEADS * IDX_DIM
TOPK_MAX = 256
T5_BUCKETS = 32
T5_MAX_DIST = 128

LANES = 128
V7X_VMEM_BYTES = 64 * 1024 * 1024
NEG = -1e30
INT_MIN = -2147483648
LOG2E = math.log2(math.e)


def _cparams(n_axes, vmem_bytes):
    limit = min(int(vmem_bytes * 1.25) + (4 << 20), V7X_VMEM_BYTES - (8 << 20))
    return pltpu.CompilerParams(dimension_semantics=("arbitrary",) * n_axes,
                                vmem_limit_bytes=limit)


def _pick_tile(n, cap):
    if n <= cap:
        return n
    best = None
    for d in range(LANES, cap + 1, LANES):
        if n % d == 0:
            best = d
    assert best is not None, (n, cap)
    return best


def _nbytes(shape, dtype):
    n = 1
    for s in shape:
        n *= s
    return n * jnp.dtype(dtype).itemsize


def _rmsnorm_body(x_ref, g_ref, o_ref):
    x = x_ref[...]
    ms = jnp.mean(x * x, axis=-1, keepdims=True)
    o_ref[...] = (x * lax.rsqrt(ms + EPS) * g_ref[...]).astype(o_ref.dtype)


def _rmsnorm(x, g, out_dtype):
    M, D = x.shape
    tm = _pick_tile(M, 256)
    vm = 2 * (_nbytes((tm, D), F32) + _nbytes((tm, D), out_dtype)) + 3 * _nbytes((tm, D), F32)
    return pl.pallas_call(
        _rmsnorm_body,
        grid=(M // tm,),
        in_specs=[pl.BlockSpec((tm, D), lambda i: (i, 0)),
                  pl.BlockSpec((1, D), lambda i: (0, 0))],
        out_specs=pl.BlockSpec((tm, D), lambda i: (i, 0)),
        out_shape=jax.ShapeDtypeStruct((M, D), out_dtype),
        compiler_params=_cparams(1, vm),
        name="rmsnorm",
    )(x, g.reshape(1, D).astype(F32))


def _matmul_body(*refs, n_pairs, has_res):
    o_ref = refs[-1]
    acc = None
    for x_ref, w_ref in zip(refs[:n_pairs], refs[n_pairs:2 * n_pairs]):
        d = jnp.dot(x_ref[...], w_ref[...], preferred_element_type=F32)
        acc = d if acc is None else acc + d
    if has_res:
        acc = acc + refs[2 * n_pairs][...]
    o_ref[...] = acc.astype(o_ref.dtype)


def _matmul(pairs, out_dtype, *, residual=None, tm_cap=512, tn_cap=1536, name="matmul"):
    M = pairs[0][0].shape[0]
    N = pairs[0][1].shape[1]
    tm = _pick_tile(M, tm_cap)
    tn = _pick_tile(N, tn_cap)
    in_specs, args, vm = [], [], 0
    for x, _ in pairs:
        K = x.shape[1]
        in_specs.append(pl.BlockSpec((tm, K), lambda j, i: (i, 0)))
        args.append(x)
        vm += 2 * _nbytes((tm, K), x.dtype)
    for _, w in pairs:
        K = w.shape[0]
        in_specs.append(pl.BlockSpec((K, tn), lambda j, i: (0, j)))
        args.append(w)
        vm += 2 * _nbytes((K, tn), w.dtype)
    if residual is not None:
        in_specs.append(pl.BlockSpec((tm, tn), lambda j, i: (i, j)))
        args.append(residual)
        vm += 2 * _nbytes((tm, tn), F32)
    vm += 2 * _nbytes((tm, tn), out_dtype) + 2 * _nbytes((tm, tn), F32)
    return pl.pallas_call(
        functools.partial(_matmul_body, n_pairs=len(pairs), has_res=residual is not None),
        grid=(N // tn, M // tm),
        in_specs=in_specs,
        out_specs=pl.BlockSpec((tm, tn), lambda j, i: (i, j)),
        out_shape=jax.ShapeDtypeStruct((M, N), out_dtype),
        compiler_params=_cparams(2, vm),
        name=name,
    )(*args)


def _ple_body(hn_ref, wg_ref, p_ref, wp_ref, h_ref, o_ref):
    z = jnp.dot(hn_ref[...], wg_ref[...], preferred_element_type=F32)
    gate = 1.0 / (1.0 + jnp.exp(-z))
    pe = jnp.dot(p_ref[...].astype(BF16), wp_ref[...], preferred_element_type=F32)
    o_ref[...] = h_ref[...] + gate * pe


def _ple(hn, wg, p_all, layer, wp, h):
    M, D = h.shape
    P = p_all.shape[-1]
    tm = _pick_tile(M, 512)
    tn = _pick_tile(D, 1024)
    vm = (2 * (_nbytes((tm, D), BF16) + _nbytes((D, tn), BF16) + _nbytes((tm, P), F32)
               + _nbytes((P, tn), BF16) + 2 * _nbytes((tm, tn), F32))
          + 3 * _nbytes((tm, tn), F32))
    return pl.pallas_call(
        _ple_body,
        grid=(D // tn, M // tm),
        in_specs=[pl.BlockSpec((tm, D), lambda j, i: (i, 0)),
                  pl.BlockSpec((D, tn), lambda j, i: (0, j)),
                  pl.BlockSpec((None, tm, P), lambda j, i: (layer, i, 0)),
                  pl.BlockSpec((P, tn), lambda j, i: (0, j)),
                  pl.BlockSpec((tm, tn), lambda j, i: (i, j))],
        out_specs=pl.BlockSpec((tm, tn), lambda j, i: (i, j)),
        out_shape=jax.ShapeDtypeStruct((M, D), F32),
        compiler_params=_cparams(2, vm),
        name="ple_gate",
    )(hn, wg, p_all, wp, h)


def _cum_body(f_ref, b_ref, o_ref, scr, *, S):
    x = f_ref[...] + b_ref[...]
    neg_log_f = jnp.maximum(-x, 0.0) + jnp.log1p(jnp.exp(-jnp.abs(x)))
    r = lax.broadcasted_iota(I32, (BLOCK, BLOCK), 0)
    c = lax.broadcasted_iota(I32, (BLOCK, BLOCK), 1)
    tri = (c <= r).astype(BF16)
    run = jnp.zeros((1, LANES), F32)
    for i in range(S // BLOCK):
        blk = neg_log_f[i * BLOCK:(i + 1) * BLOCK, :]
        hi = blk.astype(BF16)
        r1 = blk - hi.astype(F32)
        mid = r1.astype(BF16)
        lo = (r1 - mid.astype(F32)).astype(BF16)
        cs = (jnp.dot(tri, hi, preferred_element_type=F32)
              + jnp.dot(tri, mid, preferred_element_type=F32)
              + jnp.dot(tri, lo, preferred_element_type=F32)) + run
        scr[i * BLOCK:(i + 1) * BLOCK, :] = cs
        run = cs[BLOCK - 1:BLOCK, :]
    o_ref[0] = (scr[...] * LOG2E).T[:FOX_HEADS, :]


def _fox_cum(f, b_f, B, S):
    b_pad = jnp.zeros((1, LANES), F32).at[0, :FOX_HEADS].set(b_f.astype(F32))
    vm = 6 * _nbytes((S, LANES), F32)
    return pl.pallas_call(
        functools.partial(_cum_body, S=S),
        grid=(B,),
        in_specs=[pl.BlockSpec((S, LANES), lambda b: (b, 0)),
                  pl.BlockSpec((1, LANES), lambda b: (0, 0))],
        out_specs=pl.BlockSpec((1, FOX_HEADS, S), lambda b: (b, 0, 0)),
        out_shape=jax.ShapeDtypeStruct((B, FOX_HEADS, S), F32),
        scratch_shapes=[pltpu.VMEM((S, LANES), F32)],
        compiler_params=_cparams(1, vm),
        name="fox_decay_cumsum",
    )(f, b_pad)


def _flash_body(*refs, T, hb, fox):
    if fox:
        q_ref, k_ref, v_ref, g_ref, c_ref, o_ref = refs
    else:
        q_ref, k_ref, v_ref, g_ref, mb_ref, d_ref, o_ref, bias_scr, mask_scr = refs
    qi = pl.program_id(1)
    hd = LANES
    head_cols = [slice(hh * hd, (hh + 1) * hd) for hh in range(hb)]

    if not fox:
        hg = pl.program_id(2)

        @pl.when((pl.program_id(0) == 0) & (qi == 0) & (hg == 0))
        def _():
            bias_scr[...] = jnp.zeros(bias_scr.shape, F32)

        @pl.when(hg == 0)
        def _():
            def widen(c, _):
                off = pl.multiple_of(c * T, T)
                mask_scr[:, pl.ds(off, T)] = mb_ref[:, pl.ds(off, T)].astype(F32)
                return 0
            lax.fori_loop(0, qi + 1, widen, 0)

        for hh in range(hb):
            for a in range(T // BLOCK):
                lo = T + (a - 1) * BLOCK
                bias_scr[hh, a * BLOCK:(a + 1) * BLOCK, lo:lo + 2 * BLOCK] = d_ref[hh]

    qs = [q_ref[:, cols] for cols in head_cols]

    def step(j, carry, kind):
        off = pl.multiple_of(j * T, T)
        out = []
        for hh, cols in enumerate(head_cols):
            m_old, l_old, acc = carry[hh]
            k = k_ref[pl.ds(off, T), cols]
            s = lax.dot_general(qs[hh], k, (((1,), (1,)), ((), ())), preferred_element_type=F32)
            if fox:
                s = s + c_ref[0, hh:hh + 1, pl.ds(off, T)]
                if kind == "diag":
                    row = lax.broadcasted_iota(I32, (T, T), 0)
                    col = lax.broadcasted_iota(I32, (T, T), 1)
                    s = jnp.where(row >= col, s, NEG)
            else:
                s = s + mask_scr[:, pl.ds(off, T)]
                if kind == "prev":
                    s = s + bias_scr[hh, :, 0:T]
                elif kind == "diag":
                    s = s + bias_scr[hh, :, T:2 * T]
            m_new = jnp.maximum(m_old, jnp.max(s, axis=-1, keepdims=True))
            alpha = jnp.exp2(m_old - m_new)
            p = jnp.exp2(s - m_new)
            l_new = alpha * l_old + jnp.sum(p, axis=-1, keepdims=True)
            v = v_ref[pl.ds(off, T), cols]
            acc = alpha * acc + jnp.dot(p.astype(BF16), v, preferred_element_type=F32)
            out.append((m_new, l_new, acc))
        return tuple(out)

    carry = tuple((jnp.full((T, 1), NEG, F32), jnp.zeros((T, 1), F32), jnp.zeros((T, hd), F32))
                  for _ in range(hb))
    if fox:
        carry = lax.fori_loop(0, qi, functools.partial(step, kind="plain"), carry)
    else:
        n_plain = jnp.maximum(qi - 1, 0)
        carry = lax.fori_loop(0, n_plain, functools.partial(step, kind="plain"), carry)
        carry = lax.fori_loop(n_plain, qi, functools.partial(step, kind="prev"), carry)
    carry = step(qi, carry, "diag")
    for hh, cols in enumerate(head_cols):
        _, l, acc = carry[hh]
        g = g_ref[:, cols].astype(F32)
        o_ref[:, cols] = (acc / l * (g / (1.0 + jnp.exp(-g)))).astype(o_ref.dtype)


def _flash(q_arr, k_arr, v_arr, g_arr, *, B, S, H, q_blk, k_blk, v_blk, g_blk, hb, T,
           decay=None, mask_bias=None, band=None):
    fox = decay is not None
    M = B * S
    nq = S // T
    w = hb * LANES
    qb, kb, vb, gb = q_blk // hb, k_blk // hb, v_blk // hb, g_blk // hb
    in_specs = [pl.BlockSpec((T, w), lambda b, i, h: (b * nq + i, qb + h)),
                pl.BlockSpec((S, w), lambda b, i, h: (b, kb + h)),
                pl.BlockSpec((S, w), lambda b, i, h: (b, vb + h)),
                pl.BlockSpec((T, w), lambda b, i, h: (b * nq + i, gb + h))]
    args = [q_arr, k_arr, v_arr, g_arr]
    vm = (2 * (3 * _nbytes((T, w), BF16) + 2 * _nbytes((S, w), BF16))
          + 6 * hb * _nbytes((T, T), F32))
    scratch = []
    if fox:
        in_specs.append(pl.BlockSpec((1, hb, S), lambda b, i, h: (b * (H // hb) + h, 0, 0)))
        args.append(decay.reshape(B * H // hb, hb, S))
        vm += 2 * _nbytes((8, S), F32)
    else:
        in_specs.append(pl.BlockSpec((T, S), lambda b, i, h: (b * nq + i, 0)))
        in_specs.append(pl.BlockSpec((hb, BLOCK, 2 * BLOCK), lambda b, i, h: (h, 0, 0)))
        args += [mask_bias, band]
        scratch = [pltpu.VMEM((hb, T, 2 * T), F32), pltpu.VMEM((T, S), F32)]
        vm += (2 * _nbytes((T, S), BF16) + _nbytes((hb, T, 2 * T), F32) + _nbytes((T, S), F32)
               + 2 * _nbytes((hb, BLOCK, 2 * BLOCK), F32))
    return pl.pallas_call(
        functools.partial(_flash_body, T=T, hb=hb, fox=fox),
        grid=(B, nq, H // hb),
        in_specs=in_specs,
        out_specs=pl.BlockSpec((T, w), lambda b, i, h: (b * nq + i, h)),
        out_shape=jax.ShapeDtypeStruct((M, H * LANES), BF16),
        scratch_shapes=scratch,
        compiler_params=_cparams(3, vm),
        name="fox_attention" if fox else "dsa_attention",
    )(*args)


def _swa_body(sink_ref, q_ref, kp_ref, kc_ref, vp_ref, vc_ref, g_ref, bias_ref, o_ref):
    n = pl.program_id(1)
    G, dh = SWA_GROUP, SWA_HEAD_DIM
    rows = G * BLOCK
    r = lax.broadcasted_iota(I32, (rows, 2 * BLOCK), 0) & (BLOCK - 1)
    c = lax.broadcasted_iota(I32, (rows, 2 * BLOCK), 1)
    d = c - r
    first_col = jnp.where(n > 0, 0, BLOCK)
    valid = (d >= 1) & (d <= BLOCK) & (c >= first_col)
    head_of_row = lax.broadcasted_iota(I32, (rows, 1), 0) >> 7
    outs = []
    for kvh in range(SWA_KV_HEADS):
        kcols = slice(kvh * dh, (kvh + 1) * dh)
        k = jnp.concatenate([kp_ref[:, kcols], kc_ref[:, kcols]], axis=0)
        v = jnp.concatenate([vp_ref[:, kcols], vc_ref[:, kcols]], axis=0)
        qs = jnp.concatenate(
            [q_ref[:, (kvh * G + i) * dh:(kvh * G + i + 1) * dh] for i in range(G)], axis=0)
        s = lax.dot_general(qs, k, (((1,), (1,)), ((), ())), preferred_element_type=F32)
        s = s + bias_ref[kvh * G:(kvh + 1) * G].reshape(rows, 2 * BLOCK)
        s = jnp.where(valid, s, NEG)
        sink = jnp.zeros((rows, 1), F32)
        for i in range(G):
            sink = jnp.where(head_of_row == i, sink_ref[kvh * G + i], sink)
        m = jnp.maximum(jnp.max(s, axis=-1, keepdims=True), sink)
        p = jnp.exp(s - m)
        den = jnp.sum(p, axis=-1, keepdims=True) + jnp.exp(sink - m)
        o = jnp.dot(p.astype(BF16), v, preferred_element_type=F32) / den
        outs += [o[i * BLOCK:(i + 1) * BLOCK, :] for i in range(G)]
    o_all = jnp.concatenate(outs, axis=1)
    g = g_ref[...].astype(F32)
    o_ref[...] = (o_all * (g / (1.0 + jnp.exp(-g)))).astype(o_ref.dtype)


def _swa(proj, sinks, band, *, B, S, q_blk, g_blk, k_blk, v_blk):
    M = B * S
    nb = S // BLOCK
    W, KW = SWA_WIDTH, SWA_KV_WIDTH

    def prev(b, n):
        return b * nb + jnp.maximum(n - 1, 0)

    vm = (2 * (3 * _nbytes((BLOCK, W), BF16) + 4 * _nbytes((BLOCK, KW), BF16))
          + 2 * _nbytes(band.shape, F32) + 16 * _nbytes((SWA_GROUP * BLOCK, 2 * BLOCK), F32))
    return pl.pallas_call(
        _swa_body,
        grid=(B, nb),
        in_specs=[pl.BlockSpec(memory_space=pltpu.SMEM),
                  pl.BlockSpec((BLOCK, W), lambda b, n: (b * nb + n, q_blk)),
                  pl.BlockSpec((BLOCK, KW), lambda b, n: (prev(b, n), k_blk)),
                  pl.BlockSpec((BLOCK, KW), lambda b, n: (b * nb + n, k_blk)),
                  pl.BlockSpec((BLOCK, KW), lambda b, n: (prev(b, n), v_blk)),
                  pl.BlockSpec((BLOCK, KW), lambda b, n: (b * nb + n, v_blk)),
                  pl.BlockSpec((BLOCK, W), lambda b, n: (b * nb + n, g_blk)),
                  pl.BlockSpec(band.shape, lambda b, n: (0, 0, 0))],
        out_specs=pl.BlockSpec((BLOCK, W), lambda b, n: (b * nb + n, 0)),
        out_shape=jax.ShapeDtypeStruct((M, W), BF16),
        compiler_params=_cparams(2, vm),
        name="swa_sink_attention",
    )(sinks.astype(F32), proj, proj, proj, proj, proj, proj, band)


_ODD_SMALL = DSA_Q_LATENT + DSA_KV_LATENT + LANES


def _odd_post_body(x_ref, qg_ref, kvg_ref, lng_ref, lnb_ref, cq_ref, ckv_ref, kidx_ref, w_ref):
    cq = x_ref[:, 0:DSA_Q_LATENT]
    cq_ref[...] = (cq * lax.rsqrt(jnp.mean(cq * cq, axis=-1, keepdims=True) + EPS)
                   * qg_ref[...]).astype(cq_ref.dtype)
    ckv = x_ref[:, DSA_Q_LATENT:DSA_Q_LATENT + DSA_KV_LATENT]
    ckv_ref[...] = (ckv * lax.rsqrt(jnp.mean(ckv * ckv, axis=-1, keepdims=True) + EPS)
                    * kvg_ref[...]).astype(ckv_ref.dtype)
    tail = x_ref[:, DSA_Q_LATENT + DSA_KV_LATENT:_ODD_SMALL]
    ki = tail[:, 0:IDX_DIM]
    mu = jnp.mean(ki, axis=-1, keepdims=True)
    xc = ki - mu
    var = jnp.mean(xc * xc, axis=-1, keepdims=True)
    kidx_ref[...] = (xc * lax.rsqrt(var + EPS) * lng_ref[...] + lnb_ref[...]).astype(kidx_ref.dtype)
    w_ref[...] = tail[:, IDX_DIM:IDX_DIM + IDX_HEADS] * (IDX_HEADS ** -0.5 * IDX_DIM ** -0.5)


def _odd_post(small, q_norm_g, kv_norm_g, ln_g, ln_b):
    M = small.shape[0]
    tm = _pick_tile(M, 512)
    vm = 8 * _nbytes((tm, _ODD_SMALL), F32)
    row = lambda i: (i, 0)
    fixed = lambda i: (0, 0)
    return pl.pallas_call(
        _odd_post_body,
        grid=(M // tm,),
        in_specs=[pl.BlockSpec((tm, _ODD_SMALL), row),
                  pl.BlockSpec((1, DSA_Q_LATENT), fixed),
                  pl.BlockSpec((1, DSA_KV_LATENT), fixed),
                  pl.BlockSpec((1, IDX_DIM), fixed),
                  pl.BlockSpec((1, IDX_DIM), fixed)],
        out_specs=[pl.BlockSpec((tm, DSA_Q_LATENT), row),
                   pl.BlockSpec((tm, DSA_KV_LATENT), row),
                   pl.BlockSpec((tm, IDX_DIM), row),
                   pl.BlockSpec((tm, IDX_HEADS), row)],
        out_shape=[jax.ShapeDtypeStruct((M, DSA_Q_LATENT), BF16),
                   jax.ShapeDtypeStruct((M, DSA_KV_LATENT), BF16),
                   jax.ShapeDtypeStruct((M, IDX_DIM), BF16),
                   jax.ShapeDtypeStruct((M, IDX_HEADS), F32)],
        compiler_params=_cparams(1, vm),
        name="dsa_latent_norms",
    )(small, q_norm_g.reshape(1, -1).astype(F32), kv_norm_g.reshape(1, -1).astype(F32),
      ln_g.reshape(1, -1).astype(F32), ln_b.reshape(1, -1).astype(F32))


def _indexer_body(q_ref, k_ref, w_ref, o_ref, key_scr, *, tq, tk, S, topk):
    qi = pl.program_id(1)
    nck = (qi * tq) // tk + 1
    nt = tk // LANES
    rowpos = qi * tq + lax.broadcasted_iota(I32, (tq, tk), 0)
    colpos0 = lax.broadcasted_iota(I32, (tq, tk), 1)

    def score_chunk(c, _):
        off = pl.multiple_of(c * tk, tk)
        kc = k_ref[pl.ds(off, tk), :]
        acc = jnp.zeros((tq, tk), F32)
        for h in range(IDX_HEADS):
            qh = q_ref[:, h * IDX_DIM:(h + 1) * IDX_DIM]
            d = lax.dot_general(qh, kc, (((1,), (1,)), ((), ())), preferred_element_type=F32)
            acc = acc + jnp.maximum(d, 0.0) * w_ref[:, h:h + 1]
        bits = pltpu.bitcast(acc, I32)
        key = bits ^ ((bits >> 31) & 0x7FFFFFFF)
        key = jnp.where(colpos0 + off <= rowpos, key, INT_MIN)
        key_scr[:, pl.ds(off, tk)] = key
        return 0

    lax.fori_loop(0, nck, score_chunk, 0)

    def count(pred):
        def body(c, acc):
            off = pl.multiple_of(c * tk, tk)
            kk = key_scr[:, pl.ds(off, tk)]
            for t in range(nt):
                acc = acc + jnp.where(pred(kk[:, t * LANES:(t + 1) * LANES]), 1, 0)
            return acc
        part = lax.fori_loop(0, nck, body, jnp.zeros((tq, LANES), I32))
        return jnp.sum(part, axis=-1, keepdims=True)

    def bisect(i, t_u):
        cand_u = t_u | lax.shift_left(jnp.int32(1), 31 - i)
        cand = cand_u ^ INT_MIN
        cnt = count(lambda kk: kk >= cand)
        return jnp.where(cnt >= topk, cand_u, t_u)

    t_u = lax.fori_loop(0, 32, bisect, jnp.zeros((tq, 1), I32))
    thr = t_u ^ INT_MIN
    has_thr = thr > INT_MIN
    thr_sel = jnp.maximum(thr, INT_MIN + 1)
    n_ge = count(lambda kk: kk >= thr)
    tie = jnp.max(jnp.where(has_thr & (n_ge > topk), 1, 0)) > 0

    def write_rest():
        def body(c, _):
            off = pl.multiple_of(c * tk, tk)
            o_ref[:, pl.ds(off, tk)] = jnp.full((tq, tk), NEG, o_ref.dtype)
            return 0
        lax.fori_loop(nck, S // tk, body, 0)

    @pl.when(jnp.logical_not(tie))
    def _():
        def body(c, _):
            off = pl.multiple_of(c * tk, tk)
            kk = key_scr[:, pl.ds(off, tk)]
            o_ref[:, pl.ds(off, tk)] = jnp.where(kk >= thr_sel, 0.0, NEG).astype(o_ref.dtype)
            return 0
        lax.fori_loop(0, nck, body, 0)

    @pl.when(tie)
    def _():
        n_gt = count(lambda kk: kk > thr)
        room = jnp.where(has_thr, topk - n_gt, 0).astype(F32)
        r = lax.broadcasted_iota(I32, (tk, tk), 0)
        cc = lax.broadcasted_iota(I32, (tk, tk), 1)
        upper = (r <= cc).astype(BF16)

        def body(c, run):
            off = pl.multiple_of(c * tk, tk)
            kk = key_scr[:, pl.ds(off, tk)]
            eq = kk == thr
            eq_f = jnp.where(eq, 1.0, 0.0)
            rank = jnp.dot(eq_f.astype(BF16), upper, preferred_element_type=F32) + run
            sel = (kk > thr) | (eq & (rank <= room))
            o_ref[:, pl.ds(off, tk)] = jnp.where(sel, 0.0, NEG).astype(o_ref.dtype)
            return run + jnp.sum(eq_f, axis=-1, keepdims=True)
        lax.fori_loop(0, nck, body, jnp.zeros((tq, 1), F32))

    write_rest()


def _indexer(qq, kidx, widx, *, B, S, topk, q_blk):
    M = B * S
    tq = BLOCK
    tk = _pick_tile(S, 512)
    nq = S // tq
    vm = (2 * (_nbytes((tq, IDX_WIDTH), BF16) + _nbytes((S, LANES), BF16) + _nbytes((tq, LANES), F32)
               + _nbytes((tq, S), BF16)) + _nbytes((tq, S), I32) + 16 * _nbytes((tq, tk), F32))
    return pl.pallas_call(
        functools.partial(_indexer_body, tq=tq, tk=tk, S=S, topk=topk),
        grid=(B, nq),
        in_specs=[pl.BlockSpec((tq, IDX_WIDTH), lambda b, i: (b * nq + i, q_blk)),
                  pl.BlockSpec((S, IDX_DIM), lambda b, i: (b, 0)),
                  pl.BlockSpec((tq, IDX_HEADS), lambda b, i: (b * nq + i, 0))],
        out_specs=pl.BlockSpec((tq, S), lambda b, i: (b * nq + i, 0)),
        out_shape=jax.ShapeDtypeStruct((M, S), BF16),
        scratch_shapes=[pltpu.VMEM((tq, S), I32)],
        compiler_params=_cparams(2, vm),
        name="dsa_indexer_topk",
    )(qq, kidx, widx)


def _t5_bucket(rel):
    n = jnp.maximum(rel, 0)
    max_exact = T5_BUCKETS // 2
    nf = jnp.maximum(n, 1).astype(F32)
    large = max_exact + (jnp.log(nf / max_exact) / math.log(T5_MAX_DIST / max_exact)
                         * (T5_BUCKETS - max_exact)).astype(I32)
    large = jnp.minimum(large, T5_BUCKETS - 1)
    return jnp.where(n < max_exact, n, large)


def _band_tables(t5_table):
    rel = jnp.arange(BLOCK)[:, None] + BLOCK - jnp.arange(2 * BLOCK)[None, :]
    band = t5_table.astype(F32)[_t5_bucket(rel)].transpose(2, 0, 1)
    swa = band[:SWA_Q_HEADS]
    far = t5_table.astype(F32)[T5_BUCKETS - 1, SWA_Q_HEADS:]
    dsa = (band[SWA_Q_HEADS:] - far[:, None, None]) * LOG2E
    return swa, dsa


def _flash_tile(S):
    return _pick_tile(S, 512)


def _even_layer(h, hn, w_in, b_f, sinks, w_out, band_swa, B, S):
    o = [0]
    for sz in (FOX_WIDTH, FOX_WIDTH, FOX_WIDTH, FOX_HEADS, FOX_WIDTH,
               SWA_WIDTH, SWA_KV_WIDTH, SWA_KV_WIDTH, SWA_WIDTH):
        o.append(o[-1] + sz)
    seg = lambda i: w_in[:, o[i]:o[i + 1]]
    w_main = jnp.concatenate(
        [seg(0) * (FOX_HEAD_DIM ** -0.5 * LOG2E), seg(1), seg(2), seg(4),
         seg(5) * (SWA_HEAD_DIM ** -0.5), seg(8), seg(6), seg(7)], axis=1).astype(BF16)
    w_f = jnp.pad(seg(3), ((0, 0), (0, LANES - FOX_HEADS))).astype(BF16)
    proj = _matmul([(hn, w_main)], BF16, name="even_in_proj")
    f = _matmul([(hn, w_f)], F32, name="even_gate_proj")
    decay = _fox_cum(f, b_f, B, S)
    nh = FOX_HEADS
    ya = _flash(proj, proj, proj, proj, B=B, S=S, H=nh, q_blk=0, k_blk=nh, v_blk=2 * nh,
                g_blk=3 * nh, hb=4, T=_flash_tile(S), decay=decay)
    base = 4 * FOX_WIDTH
    yb = _swa(proj, sinks, band_swa, B=B, S=S,
              q_blk=base // SWA_WIDTH, g_blk=base // SWA_WIDTH + 1,
              k_blk=(base + 2 * SWA_WIDTH) // SWA_KV_WIDTH,
              v_blk=(base + 2 * SWA_WIDTH) // SWA_KV_WIDTH + 1)
    w_o = w_out.astype(BF16)
    return _matmul([(ya, w_o[:FOX_WIDTH]), (yb, w_o[FOX_WIDTH:])], F32, residual=h,
                   tn_cap=1024, name="even_out_proj")


def _odd_layer(h, hn, w_in, q_norm_g, kv_norm_g, w_uq, w_uq_idx, ln_g, ln_b, w_uk, w_uv, w_out,
               band_dsa, B, S, topk):
    n_small = DSA_Q_LATENT + DSA_KV_LATENT + IDX_DIM + IDX_HEADS
    w_small = jnp.pad(w_in[:, :n_small], ((0, 0), (0, _ODD_SMALL - n_small))).astype(BF16)
    w_g = w_in[:, n_small:].astype(BF16)
    small = _matmul([(hn, w_small)], F32, tm_cap=256, tn_cap=_ODD_SMALL, name="odd_latent_proj")
    gc = _matmul([(hn, w_g)], BF16, tn_cap=1024, name="odd_gate_proj")
    cq, ckv, kidx, widx = _odd_post(small, q_norm_g, kv_norm_g, ln_g, ln_b)
    w_q = jnp.concatenate([w_uq * (DSA_QK_DIM ** -0.5 * LOG2E), w_uq_idx], axis=1).astype(BF16)
    qq = _matmul([(cq, w_q)], BF16, tn_cap=2048, name="odd_q_up")
    L = w_uk.shape[0]
    w_kv = jnp.concatenate([w_uk.reshape(L, DSA_WIDTH), w_uv.reshape(L, DSA_WIDTH)], axis=1).astype(BF16)
    kv = _matmul([(ckv, w_kv)], BF16, tn_cap=2048, name="odd_kv_up")
    mask_bias = _indexer(qq, kidx, widx, B=B, S=S, topk=topk, q_blk=DSA_WIDTH // IDX_WIDTH)
    nh = DSA_HEADS
    y = _flash(qq, kv, kv, gc, B=B, S=S, H=nh, q_blk=0, k_blk=0, v_blk=nh, g_blk=0, hb=2,
               T=_flash_tile(S), mask_bias=mask_bias, band=band_dsa)
    return _matmul([(y, w_out.astype(BF16))], F32, residual=h, tn_cap=1024, name="odd_out_proj")


def kernel(x, p, t5_table, norm_g, even_w_in, even_b_f, even_sinks, even_w_out, odd_w_in, odd_q_norm_g, odd_kv_norm_g, odd_w_uq, odd_w_uq_idx, odd_idx_ln_g, odd_idx_ln_b, odd_w_uk, odd_w_uv, odd_w_out, ple_w_proj, ple_norm_g, ple_w_gate, final_g):
    B, S, D = x.shape
    depth = norm_g.shape[0]
    M = B * S
    topk = min(TOPK_MAX, S // 4)
    band_swa, band_dsa = _band_tables(t5_table)
    h = x.reshape(M, D).astype(F32)
    p_all = p.reshape(depth, M, p.shape[-1]).astype(F32)
    for i in range(depth):
        j = i // 2
        hn = _rmsnorm(h, norm_g[i], BF16)
        if i % 2 == 0:
            h = _even_layer(h, hn, even_w_in[j], even_b_f[j], even_sinks[j], even_w_out[j],
                            band_swa, B, S)
        else:
            h = _odd_layer(h, hn, odd_w_in[j], odd_q_norm_g[j], odd_kv_norm_g[j], odd_w_uq[j],
                           odd_w_uq_idx[j], odd_idx_ln_g[j], odd_idx_ln_b[j], odd_w_uk[j],
                           odd_w_uv[j], odd_w_out[j], band_dsa, B, S, topk)
        hn = _rmsnorm(h, ple_norm_g[i], BF16)
        h = _ple(hn, ple_w_gate[i].astype(BF16), p_all, i, ple_w_proj[i].astype(BF16), h)
    return _rmsnorm(h, final_g, F32).reshape(B, S, D)
```

```python
import functools
import math
from typing import NamedTuple

import jax
import jax.numpy as jnp
from jax import lax
from jax.experimental import pallas as pl
from jax.experimental.pallas import tpu as pltpu

F32 = jnp.float32
BF16 = jnp.bfloat16
I32 = jnp.int32

EPS = 1e-6
BLOCK = 128
PLE_DIM = 256
FOX_HEADS = 16
FOX_HEAD_DIM = 128
FOX_WIDTH = FOX_HEADS * FOX_HEAD_DIM
SWA_Q_HEADS = 32
SWA_KV_HEADS = 4
SWA_GROUP = SWA_Q_HEADS // SWA_KV_HEADS
SWA_HEAD_DIM = 64
SWA_WIDTH = SWA_Q_HEADS * SWA_HEAD_DIM
SWA_KV_WIDTH = SWA_KV_HEADS * SWA_HEAD_DIM
DSA_HEADS = 32
DSA_QK_DIM = 128
DSA_V_DIM = 128
DSA_WIDTH = DSA_HEADS * DSA_V_DIM
DSA_Q_LATENT = 1024
DSA_KV_LATENT = 512
IDX_HEADS = 32
IDX_DIM = 64
IDX_WIDTH = IDX_HEADS * IDX_DIM
TOPK_MAX = 256
T5_BUCKETS = 32
T5_MAX_DIST = 128

LANES = 128
V7X_VMEM_BYTES = 64 * 1024 * 1024
NEG = -1e30
INT_MIN = -2147483648
LOG2E = math.log2(math.e)


def _cparams(n_axes, vmem_bytes):
    limit = min(int(vmem_bytes * 1.25) + (4 << 20), V7X_VMEM_BYTES - (8 << 20))
    return pltpu.CompilerParams(dimension_semantics=("arbitrary",) * n_axes,
                                vmem_limit_bytes=limit)


def _pick_tile(n, cap):
    if n <= cap:
        return n
    best = None
    for d in range(LANES, cap + 1, LANES):
        if n % d == 0:
            best = d
    assert best is not None, (n, cap)
    return best


def _nbytes(shape, dtype):
    n = 1
    for s in shape:
        n *= s
    return n * jnp.dtype(dtype).itemsize


def _rmsnorm_body(x_ref, g_ref, o_ref):
    x = x_ref[...]
    ms = jnp.mean(x * x, axis=-1, keepdims=True)
    o_ref[...] = (x * lax.rsqrt(ms + EPS) * g_ref[...]).astype(o_ref.dtype)


def _rmsnorm(x, g, out_dtype):
    M, D = x.shape
    tm = _pick_tile(M, 256)
    vm = 2 * (_nbytes((tm, D), F32) + _nbytes((tm, D), out_dtype)) + 3 * _nbytes((tm, D), F32)
    return pl.pallas_call(
        _rmsnorm_body,
        grid=(M // tm,),
        in_specs=[pl.BlockSpec((tm, D), lambda i: (i, 0)),
                  pl.BlockSpec((1, D), lambda i: (0, 0))],
        out_specs=pl.BlockSpec((tm, D), lambda i: (i, 0)),
        out_shape=jax.ShapeDtypeStruct((M, D), out_dtype),
        compiler_params=_cparams(1, vm),
        name="rmsnorm",
    )(x, g.reshape(1, D).astype(F32))


class _Weight(NamedTuple):
    w: jax.Array
    layer: int
    row0: int = 0
    col0: int = 0
    shift: int = 0
    shift_from: int = 0
    scales: tuple = ()


def _cast_weight(j, wt, wa_ref, wb_ref, w_scr, tn):
    scale = None
    for lo, hi, val in wt.scales:
        hit = (j >= lo) & (j < hi)
        scale = jnp.where(hit, val, 1.0 if scale is None else scale)

    def store(w):
        if scale is not None:
            w = w * scale
        w_scr[...] = w.astype(BF16)

    if wt.shift == 0:
        store(wa_ref[...])
        return

    def shifted():
        store(jnp.concatenate([wa_ref[:, wt.shift:], wb_ref[:, :wt.shift]], axis=1))

    if wt.shift_from == 0:
        shifted()
    else:
        pl.when(j >= wt.shift_from)(shifted)
        pl.when(j < wt.shift_from)(lambda: store(wa_ref[...]))


def _matmul_body(*refs, wts, has_res, tn):
    n = len(wts)
    x_refs = refs[:n]
    pos = n
    w_refs = []
    for wt in wts:
        w_refs.append((refs[pos], refs[pos + 1] if wt.shift else None))
        pos += 2 if wt.shift else 1
    r_ref = refs[pos] if has_res else None
    pos += 1 if has_res else 0
    o_ref = refs[pos]
    w_scrs = refs[pos + 1:]
    j = pl.program_id(0)

    @pl.when(pl.program_id(1) == 0)
    def _():
        for wt, (wa_ref, wb_ref), w_scr in zip(wts, w_refs, w_scrs):
            _cast_weight(j, wt, wa_ref, wb_ref, w_scr, tn)

    acc = None
    for x_ref, w_scr in zip(x_refs, w_scrs):
        d = jnp.dot(x_ref[...], w_scr[...], preferred_element_type=F32)
        acc = d if acc is None else acc + d
    if has_res:
        acc = acc + r_ref[...]
    o_ref[...] = acc.astype(o_ref.dtype)


def _weight_specs(wt, K, tn):
    assert wt.row0 % K == 0 and wt.col0 % tn == 0 and tn % LANES == 0 and 0 <= wt.shift < LANES
    rb, cb, layer = wt.row0 // K, wt.col0 // tn, wt.layer
    specs = [pl.BlockSpec((None, K, tn), lambda j, i: (layer, rb, cb + j))]
    args = [wt.w]
    if wt.shift:
        lb = tn // LANES
        specs.append(pl.BlockSpec((None, K, LANES), lambda j, i: (layer, rb, (cb + j + 1) * lb)))
        args.append(wt.w)
    vm = 2 * _nbytes((K, tn + (LANES if wt.shift else 0)), F32) + _nbytes((K, tn), BF16)
    return specs, args, vm


def _matmul(xs, wts, n_out, out_dtype, *, residual=None, tm_cap=512, tn=512, out_block=None,
            name="matmul"):
    M = xs[0].shape[0]
    tm = _pick_tile(M, tm_cap)
    tn = min(tn, n_out)
    assert n_out % tn == 0
    out_block = out_block or (lambda j: j)
    in_specs, args, vm, scratch = [], [], 0, []
    for x in xs:
        K = x.shape[1]
        in_specs.append(pl.BlockSpec((tm, K), lambda j, i: (i, 0)))
        args.append(x)
        vm += 2 * _nbytes((tm, K), x.dtype)
    for x, wt in zip(xs, wts):
        specs, wargs, wvm = _weight_specs(wt, x.shape[1], tn)
        in_specs += specs
        args += wargs
        vm += wvm
        scratch.append(pltpu.VMEM((x.shape[1], tn), BF16))
    if residual is not None:
        in_specs.append(pl.BlockSpec((tm, tn), lambda j, i: (i, out_block(j))))
        args.append(residual)
        vm += 2 * _nbytes((tm, tn), F32)
    vm += 2 * _nbytes((tm, tn), out_dtype) + 2 * _nbytes((tm, tn), F32)
    return pl.pallas_call(
        functools.partial(_matmul_body, wts=tuple(wt._replace(w=None) for wt in wts),
                          has_res=residual is not None, tn=tn),
        grid=(n_out // tn, M // tm),
        in_specs=in_specs,
        out_specs=pl.BlockSpec((tm, tn), lambda j, i: (i, out_block(j))),
        out_shape=jax.ShapeDtypeStruct((M, n_out), out_dtype),
        scratch_shapes=scratch,
        compiler_params=_cparams(2, vm),
        name=name,
    )(*args)


def _ple_body(hn_ref, wg_ref, p_ref, wp_ref, h_ref, o_ref, wg_scr, wp_scr):
    @pl.when(pl.program_id(1) == 0)
    def _():
        wg_scr[...] = wg_ref[...].astype(BF16)
        wp_scr[...] = wp_ref[...].astype(BF16)

    z = jnp.dot(hn_ref[...], wg_scr[...], preferred_element_type=F32)
    gate = 1.0 / (1.0 + jnp.exp(-z))
    pe = jnp.dot(p_ref[...].astype(BF16), wp_scr[...], preferred_element_type=F32)
    o_ref[...] = h_ref[...] + gate * pe


def _ple(hn, wg_all, p_all, layer, wp_all, h):
    M, D = h.shape
    P = p_all.shape[-1]
    tm = _pick_tile(M, 512)
    tn = _pick_tile(D, 512)
    vm = (2 * (_nbytes((tm, D), BF16) + _nbytes((D, tn), F32) + _nbytes((tm, P), F32)
               + _nbytes((P, tn), F32) + 2 * _nbytes((tm, tn), F32))
          + _nbytes((D + P, tn), BF16) + 3 * _nbytes((tm, tn), F32))
    return pl.pallas_call(
        _ple_body,
        grid=(D // tn, M // tm),
        in_specs=[pl.BlockSpec((tm, D), lambda j, i: (i, 0)),
                  pl.BlockSpec((None, D, tn), lambda j, i: (layer, 0, j)),
                  pl.BlockSpec((None, tm, P), lambda j, i: (layer, i, 0)),
                  pl.BlockSpec((None, P, tn), lambda j, i: (layer, 0, j)),
                  pl.BlockSpec((tm, tn), lambda j, i: (i, j))],
        out_specs=pl.BlockSpec((tm, tn), lambda j, i: (i, j)),
        out_shape=jax.ShapeDtypeStruct((M, D), F32),
        scratch_shapes=[pltpu.VMEM((D, tn), BF16), pltpu.VMEM((P, tn), BF16)],
        compiler_params=_cparams(2, vm),
        name="ple_gate",
    )(hn, wg_all, p_all, wp_all, h)


def _cum_body(f_ref, b_ref, o_ref, scr, *, S):
    x = f_ref[...] + b_ref[...]
    neg_log_f = jnp.maximum(-x, 0.0) + jnp.log1p(jnp.exp(-jnp.abs(x)))
    r = lax.broadcasted_iota(I32, (BLOCK, BLOCK), 0)
    c = lax.broadcasted_iota(I32, (BLOCK, BLOCK), 1)
    tri = (c <= r).astype(BF16)
    run = jnp.zeros((1, LANES), F32)
    for i in range(S // BLOCK):
        blk = neg_log_f[i * BLOCK:(i + 1) * BLOCK, :]
        hi = blk.astype(BF16)
        r1 = blk - hi.astype(F32)
        mid = r1.astype(BF16)
        lo = (r1 - mid.astype(F32)).astype(BF16)
        cs = (jnp.dot(tri, hi, preferred_element_type=F32)
              + jnp.dot(tri, mid, preferred_element_type=F32)
              + jnp.dot(tri, lo, preferred_element_type=F32)) + run
        scr[i * BLOCK:(i + 1) * BLOCK, :] = cs
        run = cs[BLOCK - 1:BLOCK, :]
    o_ref[0] = (scr[...] * LOG2E).T[:FOX_HEADS, :]


def _fox_cum(f, b_f, B, S):
    b_pad = jnp.zeros((1, LANES), F32).at[0, :FOX_HEADS].set(b_f.astype(F32))
    vm = 6 * _nbytes((S, LANES), F32)
    return pl.pallas_call(
        functools.partial(_cum_body, S=S),
        grid=(B,),
        in_specs=[pl.BlockSpec((S, LANES), lambda b: (b, 0)),
                  pl.BlockSpec((1, LANES), lambda b: (0, 0))],
        out_specs=pl.BlockSpec((1, FOX_HEADS, S), lambda b: (b, 0, 0)),
        out_shape=jax.ShapeDtypeStruct((B, FOX_HEADS, S), F32),
        scratch_shapes=[pltpu.VMEM((S, LANES), F32)],
        compiler_params=_cparams(1, vm),
        name="fox_decay_cumsum",
    )(f, b_pad)


def _flash_body(*refs, T, hb, fox):
    if fox:
        q_ref, k_ref, v_ref, g_ref, c_ref, o_ref = refs
    else:
        q_ref, k_ref, v_ref, g_ref, mb_ref, d_ref, o_ref, bias_scr, mask_scr = refs
    qi = pl.program_id(1)
    hd = LANES
    head_cols = [slice(hh * hd, (hh + 1) * hd) for hh in range(hb)]

    if not fox:
        hg = pl.program_id(2)

        @pl.when((pl.program_id(0) == 0) & (qi == 0) & (hg == 0))
        def _():
            bias_scr[...] = jnp.zeros(bias_scr.shape, F32)

        @pl.when(hg == 0)
        def _():
            def widen(c, _):
                off = pl.multiple_of(c * T, T)
                mask_scr[:, pl.ds(off, T)] = mb_ref[:, pl.ds(off, T)].astype(F32)
                return 0
            lax.fori_loop(0, qi + 1, widen, 0)

        for hh in range(hb):
            for a in range(T // BLOCK):
                lo = T + (a - 1) * BLOCK
                bias_scr[hh, a * BLOCK:(a + 1) * BLOCK, lo:lo + 2 * BLOCK] = d_ref[hh]

    qs = [q_ref[:, cols] for cols in head_cols]

    def step(j, carry, kind):
        off = pl.multiple_of(j * T, T)
        out = []
        for hh, cols in enumerate(head_cols):
            m_old, l_old, acc = carry[hh]
            k = k_ref[pl.ds(off, T), cols]
            s = lax.dot_general(qs[hh], k, (((1,), (1,)), ((), ())), preferred_element_type=F32)
            if fox:
                s = s + c_ref[0, hh:hh + 1, pl.ds(off, T)]
                if kind == "diag":
                    row = lax.broadcasted_iota(I32, (T, T), 0)
                    col = lax.broadcasted_iota(I32, (T, T), 1)
                    s = jnp.where(row >= col, s, NEG)
            else:
                s = s + mask_scr[:, pl.ds(off, T)]
                if kind == "prev":
                    s = s + bias_scr[hh, :, 0:T]
                elif kind == "diag":
                    s = s + bias_scr[hh, :, T:2 * T]
            m_new = jnp.maximum(m_old, jnp.max(s, axis=-1, keepdims=True))
            alpha = jnp.exp2(m_old - m_new)
            p = jnp.exp2(s - m_new)
            l_new = alpha * l_old + jnp.sum(p, axis=-1, keepdims=True)
            v = v_ref[pl.ds(off, T), cols]
            acc = alpha * acc + jnp.dot(p.astype(BF16), v, preferred_element_type=F32)
            out.append((m_new, l_new, acc))
        return tuple(out)

    carry = tuple((jnp.full((T, 1), NEG, F32), jnp.zeros((T, 1), F32), jnp.zeros((T, hd), F32))
                  for _ in range(hb))
    if fox:
        carry = lax.fori_loop(0, qi, functools.partial(step, kind="plain"), carry)
    else:
        n_plain = jnp.maximum(qi - 1, 0)
        carry = lax.fori_loop(0, n_plain, functools.partial(step, kind="plain"), carry)
        carry = lax.fori_loop(n_plain, qi, functools.partial(step, kind="prev"), carry)
    carry = step(qi, carry, "diag")
    for hh, cols in enumerate(head_cols):
        _, l, acc = carry[hh]
        g = g_ref[:, cols].astype(F32)
        o_ref[:, cols] = (acc / l * (g / (1.0 + jnp.exp(-g)))).astype(o_ref.dtype)


def _flash(q_arr, k_arr, v_arr, g_arr, *, B, S, H, q_blk, k_blk, v_blk, g_blk, hb, T,
           decay=None, mask_bias=None, band=None):
    fox = decay is not None
    M = B * S
    nq = S // T
    w = hb * LANES
    qb, kb, vb, gb = q_blk // hb, k_blk // hb, v_blk // hb, g_blk // hb
    in_specs = [pl.BlockSpec((T, w), lambda b, i, h: (b * nq + i, qb + h)),
                pl.BlockSpec((S, w), lambda b, i, h: (b, kb + h)),
                pl.BlockSpec((S, w), lambda b, i, h: (b, vb + h)),
                pl.BlockSpec((T, w), lambda b, i, h: (b * nq + i, gb + h))]
    args = [q_arr, k_arr, v_arr, g_arr]
    vm = (2 * (3 * _nbytes((T, w), BF16) + 2 * _nbytes((S, w), BF16))
          + 6 * hb * _nbytes((T, T), F32))
    scratch = []
    if fox:
        in_specs.append(pl.BlockSpec((1, hb, S), lambda b, i, h: (b * (H // hb) + h, 0, 0)))
        args.append(decay.reshape(B * H // hb, hb, S))
        vm += 2 * _nbytes((8, S), F32)
    else:
        in_specs.append(pl.BlockSpec((T, S), lambda b, i, h: (b * nq + i, 0)))
        in_specs.append(pl.BlockSpec((hb, BLOCK, 2 * BLOCK), lambda b, i, h: (h, 0, 0)))
        args += [mask_bias, band]
        scratch = [pltpu.VMEM((hb, T, 2 * T), F32), pltpu.VMEM((T, S), F32)]
        vm += (2 * _nbytes((T, S), BF16) + _nbytes((hb, T, 2 * T), F32) + _nbytes((T, S), F32)
               + 2 * _nbytes((hb, BLOCK, 2 * BLOCK), F32))
    return pl.pallas_call(
        functools.partial(_flash_body, T=T, hb=hb, fox=fox),
        grid=(B, nq, H // hb),
        in_specs=in_specs,
        out_specs=pl.BlockSpec((T, w), lambda b, i, h: (b * nq + i, h)),
        out_shape=jax.ShapeDtypeStruct((M, H * LANES), BF16),
        scratch_shapes=scratch,
        compiler_params=_cparams(3, vm),
        name="fox_attention" if fox else "dsa_attention",
    )(*args)


def _swa_body(sink_ref, q_ref, kp_ref, kc_ref, vp_ref, vc_ref, g_ref, bias_ref, o_ref):
    n = pl.program_id(1)
    G, dh = SWA_GROUP, SWA_HEAD_DIM
    rows = G * BLOCK
    r = lax.broadcasted_iota(I32, (rows, 2 * BLOCK), 0) & (BLOCK - 1)
    c = lax.broadcasted_iota(I32, (rows, 2 * BLOCK), 1)
    d = c - r
    first_col = jnp.where(n > 0, 0, BLOCK)
    valid = (d >= 1) & (d <= BLOCK) & (c >= first_col)
    head_of_row = lax.broadcasted_iota(I32, (rows, 1), 0) >> (BLOCK.bit_length() - 1)
    outs = []
    for kvh in range(SWA_KV_HEADS):
        kcols = slice(kvh * dh, (kvh + 1) * dh)
        k = jnp.concatenate([kp_ref[:, kcols], kc_ref[:, kcols]], axis=0)
        v = jnp.concatenate([vp_ref[:, kcols], vc_ref[:, kcols]], axis=0)
        qs = jnp.concatenate(
            [q_ref[:, (kvh * G + i) * dh:(kvh * G + i + 1) * dh] for i in range(G)], axis=0)
        s = lax.dot_general(qs, k, (((1,), (1,)), ((), ())), preferred_element_type=F32)
        s = s + bias_ref[kvh * G:(kvh + 1) * G].reshape(rows, 2 * BLOCK)
        s = jnp.where(valid, s, NEG)
        sink = jnp.zeros((rows, 1), F32)
        for i in range(G):
            sink = jnp.where(head_of_row == i, sink_ref[kvh * G + i], sink)
        m = jnp.maximum(jnp.max(s, axis=-1, keepdims=True), sink)
        p = jnp.exp(s - m)
        den = jnp.sum(p, axis=-1, keepdims=True) + jnp.exp(sink - m)
        o = jnp.dot(p.astype(BF16), v, preferred_element_type=F32) / den
        outs += [o[i * BLOCK:(i + 1) * BLOCK, :] for i in range(G)]
    o_all = jnp.concatenate(outs, axis=1)
    g = g_ref[...].astype(F32)
    o_ref[...] = (o_all * (g / (1.0 + jnp.exp(-g)))).astype(o_ref.dtype)


def _swa(proj, sinks, band, *, B, S, q_blk, g_blk, k_blk, v_blk):
    M = B * S
    nb = S // BLOCK
    W, KW = SWA_WIDTH, SWA_KV_WIDTH

    def prev(b, n):
        return b * nb + jnp.maximum(n - 1, 0)

    vm = (2 * (3 * _nbytes((BLOCK, W), BF16) + 4 * _nbytes((BLOCK, KW), BF16))
          + 2 * _nbytes(band.shape, F32) + 16 * _nbytes((SWA_GROUP * BLOCK, 2 * BLOCK), F32))
    return pl.pallas_call(
        _swa_body,
        grid=(B, nb),
        in_specs=[pl.BlockSpec(memory_space=pltpu.SMEM),
                  pl.BlockSpec((BLOCK, W), lambda b, n: (b * nb + n, q_blk)),
                  pl.BlockSpec((BLOCK, KW), lambda b, n: (prev(b, n), k_blk)),
                  pl.BlockSpec((BLOCK, KW), lambda b, n: (b * nb + n, k_blk)),
                  pl.BlockSpec((BLOCK, KW), lambda b, n: (prev(b, n), v_blk)),
                  pl.BlockSpec((BLOCK, KW), lambda b, n: (b * nb + n, v_blk)),
                  pl.BlockSpec((BLOCK, W), lambda b, n: (b * nb + n, g_blk)),
                  pl.BlockSpec(band.shape, lambda b, n: (0, 0, 0))],
        out_specs=pl.BlockSpec((BLOCK, W), lambda b, n: (b * nb + n, 0)),
        out_shape=jax.ShapeDtypeStruct((M, W), BF16),
        compiler_params=_cparams(2, vm),
        name="swa_sink_attention",
    )(sinks.astype(F32), proj, proj, proj, proj, proj, proj, band)


def _odd_post_body(x_ref, t_ref, qg_ref, kvg_ref, lng_ref, lnb_ref, cq_ref, ckv_ref, kidx_ref,
                   w_ref):
    cq = x_ref[:, 0:DSA_Q_LATENT]
    cq_ref[...] = (cq * lax.rsqrt(jnp.mean(cq * cq, axis=-1, keepdims=True) + EPS)
                   * qg_ref[...]).astype(cq_ref.dtype)
    ckv = x_ref[:, DSA_Q_LATENT:DSA_Q_LATENT + DSA_KV_LATENT]
    ckv_ref[...] = (ckv * lax.rsqrt(jnp.mean(ckv * ckv, axis=-1, keepdims=True) + EPS)
                    * kvg_ref[...]).astype(ckv_ref.dtype)
    tail = t_ref[...]
    ki = tail[:, 0:IDX_DIM]
    mu = jnp.mean(ki, axis=-1, keepdims=True)
    xc = ki - mu
    var = jnp.mean(xc * xc, axis=-1, keepdims=True)
    kidx_ref[...] = (xc * lax.rsqrt(var + EPS) * lng_ref[...] + lnb_ref[...]).astype(kidx_ref.dtype)
    w_ref[...] = tail[:, IDX_DIM:IDX_DIM + IDX_HEADS] * (IDX_HEADS ** -0.5 * IDX_DIM ** -0.5)


def _odd_post(lat, tail, q_norm_g, kv_norm_g, ln_g, ln_b):
    M, n_lat = lat.shape
    tm = _pick_tile(M, 512)
    vm = 8 * _nbytes((tm, n_lat + LANES), F32)
    row = lambda i: (i, 0)
    fixed = lambda i: (0, 0)
    return pl.pallas_call(
        _odd_post_body,
        grid=(M // tm,),
        in_specs=[pl.BlockSpec((tm, n_lat), row),
                  pl.BlockSpec((tm, LANES), row),
                  pl.BlockSpec((1, DSA_Q_LATENT), fixed),
                  pl.BlockSpec((1, DSA_KV_LATENT), fixed),
                  pl.BlockSpec((1, IDX_DIM), fixed),
                  pl.BlockSpec((1, IDX_DIM), fixed)],
        out_specs=[pl.BlockSpec((tm, DSA_Q_LATENT), row),
                   pl.BlockSpec((tm, DSA_KV_LATENT), row),
                   pl.BlockSpec((tm, IDX_DIM), row),
                   pl.BlockSpec((tm, IDX_HEADS), row)],
        out_shape=[jax.ShapeDtypeStruct((M, DSA_Q_LATENT), BF16),
                   jax.ShapeDtypeStruct((M, DSA_KV_LATENT), BF16),
                   jax.ShapeDtypeStruct((M, IDX_DIM), BF16),
                   jax.ShapeDtypeStruct((M, IDX_HEADS), F32)],
        compiler_params=_cparams(1, vm),
        name="dsa_latent_norms",
    )(lat, tail, q_norm_g.reshape(1, -1).astype(F32), kv_norm_g.reshape(1, -1).astype(F32),
      ln_g.reshape(1, -1).astype(F32), ln_b.reshape(1, -1).astype(F32))


def _indexer_body(q_ref, k_ref, w_ref, o_ref, key_scr, *, tq, tk, S, topk):
    qi = pl.program_id(1)
    nck = (qi * tq) // tk + 1
    nt = tk // LANES
    rowpos = qi * tq + lax.broadcasted_iota(I32, (tq, tk), 0)
    colpos0 = lax.broadcasted_iota(I32, (tq, tk), 1)

    def score_chunk(c, _):
        off = pl.multiple_of(c * tk, tk)
        kc = k_ref[pl.ds(off, tk), :]
        acc = jnp.zeros((tq, tk), F32)
        for h in range(IDX_HEADS):
            qh = q_ref[:, h * IDX_DIM:(h + 1) * IDX_DIM]
            d = lax.dot_general(qh, kc, (((1,), (1,)), ((), ())), preferred_element_type=F32)
            acc = acc + jnp.maximum(d, 0.0) * w_ref[:, h:h + 1]
        bits = pltpu.bitcast(acc, I32)
        key = bits ^ ((bits >> 31) & 0x7FFFFFFF)
        key = jnp.where(colpos0 + off <= rowpos, key, INT_MIN)
        key_scr[:, pl.ds(off, tk)] = key
        return 0

    lax.fori_loop(0, nck, score_chunk, 0)

    def count(pred):
        def body(c, acc):
            off = pl.multiple_of(c * tk, tk)
            kk = key_scr[:, pl.ds(off, tk)]
            for t in range(nt):
                acc = acc + jnp.where(pred(kk[:, t * LANES:(t + 1) * LANES]), 1, 0)
            return acc
        part = lax.fori_loop(0, nck, body, jnp.zeros((tq, LANES), I32))
        return jnp.sum(part, axis=-1, keepdims=True)

    def bisect(i, t_u):
        cand_u = t_u | lax.shift_left(jnp.int32(1), 31 - i)
        cand = cand_u ^ INT_MIN
        cnt = count(lambda kk: kk >= cand)
        return jnp.where(cnt >= topk, cand_u, t_u)

    t_u = lax.fori_loop(0, 32, bisect, jnp.zeros((tq, 1), I32))
    thr = t_u ^ INT_MIN
    has_thr = thr > INT_MIN
    thr_sel = jnp.maximum(thr, INT_MIN + 1)
    n_ge = count(lambda kk: kk >= thr)
    tie = jnp.max(jnp.where(has_thr & (n_ge > topk), 1, 0)) > 0

    def write_rest():
        def body(c, _):
            off = pl.multiple_of(c * tk, tk)
            o_ref[:, pl.ds(off, tk)] = jnp.full((tq, tk), NEG, o_ref.dtype)
            return 0
        lax.fori_loop(nck, S // tk, body, 0)

    @pl.when(jnp.logical_not(tie))
    def _():
        def body(c, _):
            off = pl.multiple_of(c * tk, tk)
            kk = key_scr[:, pl.ds(off, tk)]
            o_ref[:, pl.ds(off, tk)] = jnp.where(kk >= thr_sel, 0.0, NEG).astype(o_ref.dtype)
            return 0
        lax.fori_loop(0, nck, body, 0)

    @pl.when(tie)
    def _():
        n_gt = count(lambda kk: kk > thr)
        room = jnp.where(has_thr, topk - n_gt, 0).astype(F32)
        r = lax.broadcasted_iota(I32, (tk, tk), 0)
        cc = lax.broadcasted_iota(I32, (tk, tk), 1)
        upper = (r <= cc).astype(BF16)

        def body(c, run):
            off = pl.multiple_of(c * tk, tk)
            kk = key_scr[:, pl.ds(off, tk)]
            eq = kk == thr
            eq_f = jnp.where(eq, 1.0, 0.0)
            rank = jnp.dot(eq_f.astype(BF16), upper, preferred_element_type=F32) + run
            sel = (kk > thr) | (eq & (rank <= room))
            o_ref[:, pl.ds(off, tk)] = jnp.where(sel, 0.0, NEG).astype(o_ref.dtype)
            return run + jnp.sum(eq_f, axis=-1, keepdims=True)
        lax.fori_loop(0, nck, body, jnp.zeros((tq, 1), F32))

    write_rest()


def _indexer(qq, kidx, widx, *, B, S, topk, q_blk):
    M = B * S
    tq = BLOCK
    tk = _pick_tile(S, 512)
    nq = S // tq
    vm = (2 * (_nbytes((tq, IDX_WIDTH), BF16) + _nbytes((S, LANES), BF16) + _nbytes((tq, LANES), F32)
               + _nbytes((tq, S), BF16)) + _nbytes((tq, S), I32) + 16 * _nbytes((tq, tk), F32))
    return pl.pallas_call(
        functools.partial(_indexer_body, tq=tq, tk=tk, S=S, topk=topk),
        grid=(B, nq),
        in_specs=[pl.BlockSpec((tq, IDX_WIDTH), lambda b, i: (b * nq + i, q_blk)),
                  pl.BlockSpec((S, IDX_DIM), lambda b, i: (b, 0)),
                  pl.BlockSpec((tq, IDX_HEADS), lambda b, i: (b * nq + i, 0))],
        out_specs=pl.BlockSpec((tq, S), lambda b, i: (b * nq + i, 0)),
        out_shape=jax.ShapeDtypeStruct((M, S), BF16),
        scratch_shapes=[pltpu.VMEM((tq, S), I32)],
        compiler_params=_cparams(2, vm),
        name="dsa_indexer_topk",
    )(qq, kidx, widx)


def _t5_bucket(rel):
    n = jnp.maximum(rel, 0)
    max_exact = T5_BUCKETS // 2
    nf = jnp.maximum(n, 1).astype(F32)
    large = max_exact + (jnp.log(nf / max_exact) / math.log(T5_MAX_DIST / max_exact)
                         * (T5_BUCKETS - max_exact)).astype(I32)
    large = jnp.minimum(large, T5_BUCKETS - 1)
    return jnp.where(n < max_exact, n, large)


def _band_tables(t5_table):
    rel = jnp.arange(BLOCK)[:, None] + BLOCK - jnp.arange(2 * BLOCK)[None, :]
    band = t5_table.astype(F32)[_t5_bucket(rel)].transpose(2, 0, 1)
    swa = band[:SWA_Q_HEADS]
    far = t5_table.astype(F32)[T5_BUCKETS - 1, SWA_Q_HEADS:]
    dsa = (band[SWA_Q_HEADS:] - far[:, None, None]) * LOG2E
    return swa, dsa


def _flash_tile(S):
    return _pick_tile(S, 512)


PROJ_TILE = 512


def _even_layer(h, hn, w_in, b_f, sinks, w_out, layer, band_swa, B, S):
    tn = PROJ_TILE
    f_col = 3 * FOX_WIDTH
    n_main = 4 * FOX_WIDTH + 2 * SWA_WIDTH + 2 * SWA_KV_WIDTH
    assert f_col % tn == 0 and FOX_WIDTH % tn == 0 and SWA_WIDTH % tn == 0 and 2 * SWA_KV_WIDTH == tn
    blk = lambda cols: cols // tn
    qb_lo = blk(4 * FOX_WIDTH)
    kv_blk = blk(4 * FOX_WIDTH + SWA_WIDTH)
    last = blk(n_main) - 1

    def out_block(j):
        return jnp.where(j == kv_blk, last, jnp.where(j > kv_blk, j - 1, j))

    w_main = _Weight(w_in, layer, shift=FOX_HEADS, shift_from=blk(f_col),
                     scales=((0, blk(FOX_WIDTH), FOX_HEAD_DIM ** -0.5 * LOG2E),
                             (qb_lo, qb_lo + blk(SWA_WIDTH), SWA_HEAD_DIM ** -0.5)))
    proj = _matmul([hn], [w_main], n_main, BF16, tn=tn, out_block=out_block, name="even_in_proj")
    f = _matmul([hn], [_Weight(w_in, layer, col0=f_col)], LANES, F32, tn=LANES,
                name="even_gate_proj")
    decay = _fox_cum(f, b_f, B, S)
    nh = FOX_HEADS
    ya = _flash(proj, proj, proj, proj, B=B, S=S, H=nh, q_blk=0, k_blk=nh, v_blk=2 * nh,
                g_blk=3 * nh, hb=4, T=_flash_tile(S), decay=decay)
    base = 4 * FOX_WIDTH
    yb = _swa(proj, sinks, band_swa, B=B, S=S,
              q_blk=base // SWA_WIDTH, g_blk=base // SWA_WIDTH + 1,
              k_blk=(base + 2 * SWA_WIDTH) // SWA_KV_WIDTH,
              v_blk=(base + 2 * SWA_WIDTH) // SWA_KV_WIDTH + 1)
    return _matmul([ya, yb], [_Weight(w_out, layer), _Weight(w_out, layer, row0=FOX_WIDTH)],
                   h.shape[1], F32, residual=h, tn=tn, name="even_out_proj")


def _odd_layer(h, hn, w_in, q_norm_g, kv_norm_g, w_uq, w_uq_idx, ln_g, ln_b, w_uk, w_uv, w_out,
               layer, band_dsa, B, S, topk):
    tn = PROJ_TILE
    n_lat = DSA_Q_LATENT + DSA_KV_LATENT
    n_small = n_lat + IDX_DIM + IDX_HEADS
    assert n_lat % tn == 0 and n_small - n_lat <= LANES
    lat = _matmul([hn], [_Weight(w_in, layer)], n_lat, F32, tn=tn, name="odd_latent_proj")
    tail = _matmul([hn], [_Weight(w_in, layer, col0=n_lat)], LANES, F32, tn=LANES,
                   name="odd_index_proj")
    gc = _matmul([hn], [_Weight(w_in, layer, col0=n_lat, shift=n_small - n_lat)], DSA_WIDTH, BF16,
                 tn=tn, name="odd_gate_proj")
    cq, ckv, kidx, widx = _odd_post(lat, tail, q_norm_g, kv_norm_g, ln_g, ln_b)
    tq = 2 * tn
    w_q = _Weight(w_uq, layer, scales=((0, DSA_WIDTH // tq, DSA_QK_DIM ** -0.5 * LOG2E),))
    q = _matmul([cq], [w_q], DSA_WIDTH, BF16, tn=tq, name="odd_q_up")
    q_idx = _matmul([cq], [_Weight(w_uq_idx, layer)], IDX_WIDTH, BF16, tn=tq, name="odd_qidx_up")
    n_odd, L = w_uk.shape[:2]
    k = _matmul([ckv], [_Weight(w_uk.reshape(n_odd, L, DSA_WIDTH), layer)], DSA_WIDTH, BF16,
                tn=tq, name="odd_k_up")
    v = _matmul([ckv], [_Weight(w_uv.reshape(n_odd, L, DSA_WIDTH), layer)], DSA_WIDTH, BF16,
                tn=tq, name="odd_v_up")
    mask_bias = _indexer(q_idx, kidx, widx, B=B, S=S, topk=topk, q_blk=0)
    y = _flash(q, k, v, gc, B=B, S=S, H=DSA_HEADS, q_blk=0, k_blk=0, v_blk=0, g_blk=0, hb=2,
               T=_flash_tile(S), mask_bias=mask_bias, band=band_dsa)
    return _matmul([y], [_Weight(w_out, layer)], h.shape[1], F32, residual=h, tn=tn,
                   name="odd_out_proj")


def kernel(x, p, t5_table, norm_g, even_w_in, even_b_f, even_sinks, even_w_out, odd_w_in, odd_q_norm_g, odd_kv_norm_g, odd_w_uq, odd_w_uq_idx, odd_idx_ln_g, odd_idx_ln_b, odd_w_uk, odd_w_uv, odd_w_out, ple_w_proj, ple_norm_g, ple_w_gate, final_g):
    B, S, D = x.shape
    depth = norm_g.shape[0]
    M = B * S
    topk = min(TOPK_MAX, S // 4)
    band_swa, band_dsa = _band_tables(t5_table)
    h = x.reshape(M, D).astype(F32)
    p_all = p.reshape(depth, M, p.shape[-1]).astype(F32)
    for i in range(depth):
        j = i // 2
        hn = _rmsnorm(h, norm_g[i], BF16)
        if i % 2 == 0:
            h = _even_layer(h, hn, even_w_in, even_b_f[j], even_sinks[j], even_w_out, j,
                            band_swa, B, S)
        else:
            h = _odd_layer(h, hn, odd_w_in, odd_q_norm_g[j], odd_kv_norm_g[j], odd_w_uq,
                           odd_w_uq_idx, odd_idx_ln_g[j], odd_idx_ln_b[j], odd_w_uk,
                           odd_w_uv, odd_w_out, j, band_dsa, B, S, topk)
        hn = _rmsnorm(h, ple_norm_g[i], BF16)
        h = _ple(hn, ple_w_gate, p_all, i, ple_w_proj, h)
    return _rmsnorm(h, final_g, F32).reshape(B, S, D)
```

```python
import functools
import math
from typing import NamedTuple

import jax
import jax.numpy as jnp
from jax import lax
from jax.experimental import pallas as pl
from jax.experimental.pallas import tpu as pltpu

F32 = jnp.float32
BF16 = jnp.bfloat16
I32 = jnp.int32

EPS = 1e-6
BLOCK = 128
PLE_DIM = 256
FOX_HEADS = 16
FOX_HEAD_DIM = 128
FOX_WIDTH = FOX_HEADS * FOX_HEAD_DIM
SWA_Q_HEADS = 32
SWA_KV_HEADS = 4
SWA_GROUP = SWA_Q_HEADS // SWA_KV_HEADS
SWA_HEAD_DIM = 64
SWA_WIDTH = SWA_Q_HEADS * SWA_HEAD_DIM
SWA_KV_WIDTH = SWA_KV_HEADS * SWA_HEAD_DIM
DSA_HEADS = 32
DSA_QK_DIM = 128
DSA_V_DIM = 128
DSA_WIDTH = DSA_HEADS * DSA_V_DIM
DSA_Q_LATENT = 1024
DSA_KV_LATENT = 512
IDX_HEADS = 32
IDX_DIM = 64
IDX_WIDTH = IDX_HEADS * IDX_DIM
TOPK_MAX = 256
T5_BUCKETS = 32
T5_MAX_DIST = 128

LANES = 128
SUBLANES = 8
V7X_VMEM_BYTES = 64 * 1024 * 1024
PROJ_TILE = 512
NEG = -1e30
INT_MIN = -2147483648
LOG2E = math.log2(math.e)


def _cparams(n_axes, vmem_bytes):
    limit = min(int(vmem_bytes * 1.25) + (4 << 20), V7X_VMEM_BYTES - (8 << 20))
    return pltpu.CompilerParams(dimension_semantics=("arbitrary",) * n_axes,
                                vmem_limit_bytes=limit)


def _pick_tile(n, cap):
    if n <= cap:
        return n
    best = None
    for d in range(LANES, cap + 1, LANES):
        if n % d == 0:
            best = d
    assert best is not None, (n, cap)
    return best


def _nbytes(shape, dtype):
    n = 1
    for s in shape:
        n *= s
    return n * jnp.dtype(dtype).itemsize


def _rmsnorm_body(x_ref, g_ref, o_ref):
    x = x_ref[...]
    ms = jnp.mean(x * x, axis=-1, keepdims=True)
    o_ref[...] = (x * lax.rsqrt(ms + EPS) * g_ref[...]).astype(o_ref.dtype)


def _rmsnorm(x, g, out_dtype):
    M, D = x.shape
    tm = _pick_tile(M, 256)
    vm = 2 * (_nbytes((tm, D), F32) + _nbytes((tm, D), out_dtype)) + 3 * _nbytes((tm, D), F32)
    return pl.pallas_call(
        _rmsnorm_body,
        grid=(M // tm,),
        in_specs=[pl.BlockSpec((tm, D), lambda i: (i, 0)),
                  pl.BlockSpec((1, D), lambda i: (0, 0))],
        out_specs=pl.BlockSpec((tm, D), lambda i: (i, 0)),
        out_shape=jax.ShapeDtypeStruct((M, D), out_dtype),
        compiler_params=_cparams(1, vm),
        name="rmsnorm",
    )(x, g.reshape(1, D).astype(F32))


class _Weight(NamedTuple):
    w: jax.Array
    layer: int
    row0: int = 0
    col0: int = 0
    transposed: bool = False
    skip: int = 0
    skip_from: int = 0
    scales: tuple = ()


def _matmul_body(*refs, wts, has_res):
    n = len(wts)
    x_refs, w_refs = refs[:n], refs[n:2 * n]
    r_ref = refs[2 * n] if has_res else None
    o_ref = refs[2 * n + (1 if has_res else 0)]
    w_scrs = refs[2 * n + (2 if has_res else 1):]
    j = pl.program_id(0)

    @pl.when(pl.program_id(1) == 0)
    def _():
        for wt, w_ref, w_scr in zip(wts, w_refs, w_scrs):
            w = w_ref[0] if wt.transposed else w_ref[...]
            scale = None
            for lo, hi, val in wt.scales:
                scale = jnp.where((j >= lo) & (j < hi), val, 1.0 if scale is None else scale)
            if scale is not None:
                w = w * scale
            w_scr[...] = w.astype(BF16)

    acc = None
    for wt, x_ref, w_scr in zip(wts, x_refs, w_scrs):
        contract = (((1,), (1,)), ((), ())) if wt.transposed else (((1,), (0,)), ((), ()))
        d = lax.dot_general(x_ref[...], w_scr[...], contract, preferred_element_type=F32)
        acc = d if acc is None else acc + d
    if has_res:
        acc = acc + r_ref[...]
    o_ref[...] = acc.astype(o_ref.dtype)


def _weight_spec(wt, K, tn):
    layer = wt.layer
    if wt.transposed:
        assert wt.row0 == 0 and wt.col0 % SUBLANES == 0 and wt.skip % SUBLANES == 0

        def start(j):
            skip = jnp.where(j >= wt.skip_from, wt.skip, 0) if wt.skip else 0
            return pl.multiple_of(wt.col0 + j * tn + skip, SUBLANES)
        dims = (pl.Element(1), pl.Element(tn), pl.Element(K))
        return pl.BlockSpec(dims, lambda j, i: (layer, start(j), 0)), (tn, K)
    assert wt.row0 % K == 0 and wt.col0 % tn == 0 and tn % LANES == 0 and wt.skip == 0
    rb, cb = wt.row0 // K, wt.col0 // tn
    return pl.BlockSpec((None, K, tn), lambda j, i: (layer, rb, cb + j)), (K, tn)


def _matmul(xs, wts, n_out, out_dtype, *, residual=None, tm_cap=1024, tn=512, out_block=None,
            name="matmul"):
    M = xs[0].shape[0]
    tm = _pick_tile(M, tm_cap)
    tn = min(tn, n_out)
    assert n_out % tn == 0
    out_block = out_block or (lambda j: j)
    in_specs, args, vm, scratch = [], [], 0, []
    for x in xs:
        K = x.shape[1]
        in_specs.append(pl.BlockSpec((tm, K), lambda j, i: (i, 0)))
        args.append(x)
        vm += 2 * _nbytes((tm, K), x.dtype)
    for x, wt in zip(xs, wts):
        spec, tile = _weight_spec(wt, x.shape[1], tn)
        in_specs.append(spec)
        args.append(wt.w)
        vm += 2 * _nbytes(tile, F32) + _nbytes(tile, BF16)
        scratch.append(pltpu.VMEM(tile, BF16))
    if residual is not None:
        in_specs.append(pl.BlockSpec((tm, tn), lambda j, i: (i, out_block(j))))
        args.append(residual)
        vm += 2 * _nbytes((tm, tn), F32)
    vm += 2 * _nbytes((tm, tn), out_dtype) + 2 * _nbytes((tm, tn), F32)
    return pl.pallas_call(
        functools.partial(_matmul_body, wts=tuple(wt._replace(w=None) for wt in wts),
                          has_res=residual is not None),
        grid=(n_out // tn, M // tm),
        in_specs=in_specs,
        out_specs=pl.BlockSpec((tm, tn), lambda j, i: (i, out_block(j))),
        out_shape=jax.ShapeDtypeStruct((M, n_out), out_dtype),
        scratch_shapes=scratch,
        compiler_params=_cparams(2, vm),
        name=name,
    )(*args)


def _ple_body(hn_ref, wg_ref, p_ref, wp_ref, h_ref, o_ref, wg_scr, wp_scr):
    @pl.when(pl.program_id(1) == 0)
    def _():
        wg_scr[...] = wg_ref[...].astype(BF16)
        wp_scr[...] = wp_ref[...].astype(BF16)

    z = jnp.dot(hn_ref[...], wg_scr[...], preferred_element_type=F32)
    gate = 1.0 / (1.0 + jnp.exp(-z))
    pe = jnp.dot(p_ref[...].astype(BF16), wp_scr[...], preferred_element_type=F32)
    o_ref[...] = h_ref[...] + gate * pe


def _ple(hn, wg_all, p_all, layer, wp_all, h):
    M, D = h.shape
    P = p_all.shape[-1]
    tm = _pick_tile(M, 1024)
    tn = _pick_tile(D, PROJ_TILE)
    vm = (2 * (_nbytes((tm, D), BF16) + _nbytes((D, tn), F32) + _nbytes((tm, P), F32)
               + _nbytes((P, tn), F32) + 2 * _nbytes((tm, tn), F32))
          + _nbytes((D + P, tn), BF16) + 3 * _nbytes((tm, tn), F32))
    return pl.pallas_call(
        _ple_body,
        grid=(D // tn, M // tm),
        in_specs=[pl.BlockSpec((tm, D), lambda j, i: (i, 0)),
                  pl.BlockSpec((None, D, tn), lambda j, i: (layer, 0, j)),
                  pl.BlockSpec((None, tm, P), lambda j, i: (layer, i, 0)),
                  pl.BlockSpec((None, P, tn), lambda j, i: (layer, 0, j)),
                  pl.BlockSpec((tm, tn), lambda j, i: (i, j))],
        out_specs=pl.BlockSpec((tm, tn), lambda j, i: (i, j)),
        out_shape=jax.ShapeDtypeStruct((M, D), F32),
        scratch_shapes=[pltpu.VMEM((D, tn), BF16), pltpu.VMEM((P, tn), BF16)],
        compiler_params=_cparams(2, vm),
        name="ple_gate",
    )(hn, wg_all, p_all, wp_all, h)


def _cum_body(f_ref, b_ref, o_ref, scr, *, S):
    x = f_ref[...] + b_ref[...]
    neg_log_f = jnp.maximum(-x, 0.0) + jnp.log1p(jnp.exp(-jnp.abs(x)))
    r = lax.broadcasted_iota(I32, (BLOCK, BLOCK), 0)
    c = lax.broadcasted_iota(I32, (BLOCK, BLOCK), 1)
    tri = (c <= r).astype(BF16)
    run = jnp.zeros((1, LANES), F32)
    for i in range(S // BLOCK):
        blk = neg_log_f[i * BLOCK:(i + 1) * BLOCK, :]
        hi = blk.astype(BF16)
        r1 = blk - hi.astype(F32)
        mid = r1.astype(BF16)
        lo = (r1 - mid.astype(F32)).astype(BF16)
        cs = (jnp.dot(tri, hi, preferred_element_type=F32)
              + jnp.dot(tri, mid, preferred_element_type=F32)
              + jnp.dot(tri, lo, preferred_element_type=F32)) + run
        scr[i * BLOCK:(i + 1) * BLOCK, :] = cs
        run = cs[BLOCK - 1:BLOCK, :]
    o_ref[0] = (scr[...] * LOG2E).T[:FOX_HEADS, :]


def _fox_cum(f, b_f, B, S):
    b_pad = jnp.zeros((1, LANES), F32).at[0, :FOX_HEADS].set(b_f.astype(F32))
    vm = 6 * _nbytes((S, LANES), F32)
    return pl.pallas_call(
        functools.partial(_cum_body, S=S),
        grid=(B,),
        in_specs=[pl.BlockSpec((S, LANES), lambda b: (b, 0)),
                  pl.BlockSpec((1, LANES), lambda b: (0, 0))],
        out_specs=pl.BlockSpec((1, FOX_HEADS, S), lambda b: (b, 0, 0)),
        out_shape=jax.ShapeDtypeStruct((B, FOX_HEADS, S), F32),
        scratch_shapes=[pltpu.VMEM((S, LANES), F32)],
        compiler_params=_cparams(1, vm),
        name="fox_decay_cumsum",
    )(f, b_pad)


def _flash_body(*refs, T, hb, fox):
    if fox:
        q_ref, k_ref, v_ref, g_ref, c_ref, o_ref = refs
    else:
        q_ref, k_ref, v_ref, g_ref, mb_ref, d_ref, o_ref, bias_scr, mask_scr = refs
    qi = pl.program_id(1)
    hd = LANES
    head_cols = [slice(hh * hd, (hh + 1) * hd) for hh in range(hb)]

    if not fox:
        hg = pl.program_id(2)

        @pl.when((pl.program_id(0) == 0) & (qi == 0) & (hg == 0))
        def _():
            bias_scr[...] = jnp.zeros(bias_scr.shape, F32)

        @pl.when(hg == 0)
        def _():
            def widen(c, _):
                off = pl.multiple_of(c * T, T)
                mask_scr[:, pl.ds(off, T)] = mb_ref[:, pl.ds(off, T)].astype(F32)
                return 0
            lax.fori_loop(0, qi + 1, widen, 0)

        for hh in range(hb):
            for a in range(T // BLOCK):
                lo = T + (a - 1) * BLOCK
                bias_scr[hh, a * BLOCK:(a + 1) * BLOCK, lo:lo + 2 * BLOCK] = d_ref[hh]

    qs = [q_ref[:, cols] for cols in head_cols]

    def step(j, carry, kind):
        off = pl.multiple_of(j * T, T)
        out = []
        for hh, cols in enumerate(head_cols):
            m_old, l_old, acc = carry[hh]
            k = k_ref[pl.ds(off, T), cols]
            s = lax.dot_general(qs[hh], k, (((1,), (1,)), ((), ())), preferred_element_type=F32)
            if fox:
                s = s + c_ref[0, hh:hh + 1, pl.ds(off, T)]
                if kind == "diag":
                    row = lax.broadcasted_iota(I32, (T, T), 0)
                    col = lax.broadcasted_iota(I32, (T, T), 1)
                    s = jnp.where(row >= col, s, NEG)
            else:
                s = s + mask_scr[:, pl.ds(off, T)]
                if kind == "prev":
                    s = s + bias_scr[hh, :, 0:T]
                elif kind == "diag":
                    s = s + bias_scr[hh, :, T:2 * T]
            m_new = jnp.maximum(m_old, jnp.max(s, axis=-1, keepdims=True))
            alpha = jnp.exp2(m_old - m_new)
            p = jnp.exp2(s - m_new)
            l_new = alpha * l_old + jnp.sum(p, axis=-1, keepdims=True)
            v = v_ref[pl.ds(off, T), cols]
            acc = alpha * acc + jnp.dot(p.astype(BF16), v, preferred_element_type=F32)
            out.append((m_new, l_new, acc))
        return tuple(out)

    carry = tuple((jnp.full((T, 1), NEG, F32), jnp.zeros((T, 1), F32), jnp.zeros((T, hd), F32))
                  for _ in range(hb))
    if fox:
        carry = lax.fori_loop(0, qi, functools.partial(step, kind="plain"), carry)
    else:
        n_plain = jnp.maximum(qi - 1, 0)
        carry = lax.fori_loop(0, n_plain, functools.partial(step, kind="plain"), carry)
        carry = lax.fori_loop(n_plain, qi, functools.partial(step, kind="prev"), carry)
    carry = step(qi, carry, "diag")
    for hh, cols in enumerate(head_cols):
        _, l, acc = carry[hh]
        g = g_ref[:, cols].astype(F32)
        o_ref[:, cols] = (acc / l * (g / (1.0 + jnp.exp(-g)))).astype(o_ref.dtype)


def _flash(q_arr, k_arr, v_arr, g_arr, *, B, S, H, q_blk, k_blk, v_blk, g_blk, hb, T,
           decay=None, mask_bias=None, band=None):
    fox = decay is not None
    M = B * S
    nq = S // T
    w = hb * LANES
    qb, kb, vb, gb = q_blk // hb, k_blk // hb, v_blk // hb, g_blk // hb
    in_specs = [pl.BlockSpec((T, w), lambda b, i, h: (b * nq + i, qb + h)),
                pl.BlockSpec((S, w), lambda b, i, h: (b, kb + h)),
                pl.BlockSpec((S, w), lambda b, i, h: (b, vb + h)),
                pl.BlockSpec((T, w), lambda b, i, h: (b * nq + i, gb + h))]
    args = [q_arr, k_arr, v_arr, g_arr]
    vm = (2 * (3 * _nbytes((T, w), BF16) + 2 * _nbytes((S, w), BF16))
          + 6 * hb * _nbytes((T, T), F32))
    scratch = []
    if fox:
        in_specs.append(pl.BlockSpec((1, hb, S), lambda b, i, h: (b * (H // hb) + h, 0, 0)))
        args.append(decay.reshape(B * H // hb, hb, S))
        vm += 2 * _nbytes((8, S), F32)
    else:
        in_specs.append(pl.BlockSpec((T, S), lambda b, i, h: (b * nq + i, 0)))
        in_specs.append(pl.BlockSpec((hb, BLOCK, 2 * BLOCK), lambda b, i, h: (h, 0, 0)))
        args += [mask_bias, band]
        scratch = [pltpu.VMEM((hb, T, 2 * T), F32), pltpu.VMEM((T, S), F32)]
        vm += (2 * _nbytes((T, S), BF16) + _nbytes((hb, T, 2 * T), F32) + _nbytes((T, S), F32)
               + 2 * _nbytes((hb, BLOCK, 2 * BLOCK), F32))
    return pl.pallas_call(
        functools.partial(_flash_body, T=T, hb=hb, fox=fox),
        grid=(B, nq, H // hb),
        in_specs=in_specs,
        out_specs=pl.BlockSpec((T, w), lambda b, i, h: (b * nq + i, h)),
        out_shape=jax.ShapeDtypeStruct((M, H * LANES), BF16),
        scratch_shapes=scratch,
        compiler_params=_cparams(3, vm),
        name="fox_attention" if fox else "dsa_attention",
    )(*args)


def _swa_body(sink_ref, q_ref, kp_ref, kc_ref, vp_ref, vc_ref, g_ref, bias_ref, o_ref):
    n = pl.program_id(1)
    G, dh = SWA_GROUP, SWA_HEAD_DIM
    rows = G * BLOCK
    r = lax.broadcasted_iota(I32, (rows, 2 * BLOCK), 0) & (BLOCK - 1)
    c = lax.broadcasted_iota(I32, (rows, 2 * BLOCK), 1)
    d = c - r
    first_col = jnp.where(n > 0, 0, BLOCK)
    valid = (d >= 1) & (d <= BLOCK) & (c >= first_col)
    head_of_row = lax.broadcasted_iota(I32, (rows, 1), 0) >> (BLOCK.bit_length() - 1)
    outs = []
    for kvh in range(SWA_KV_HEADS):
        kcols = slice(kvh * dh, (kvh + 1) * dh)
        k = jnp.concatenate([kp_ref[:, kcols], kc_ref[:, kcols]], axis=0)
        v = jnp.concatenate([vp_ref[:, kcols], vc_ref[:, kcols]], axis=0)
        qs = jnp.concatenate(
            [q_ref[:, (kvh * G + i) * dh:(kvh * G + i + 1) * dh] for i in range(G)], axis=0)
        s = lax.dot_general(qs, k, (((1,), (1,)), ((), ())), preferred_element_type=F32)
        s = s + bias_ref[kvh * G:(kvh + 1) * G].reshape(rows, 2 * BLOCK)
        s = jnp.where(valid, s, NEG)
        sink = jnp.zeros((rows, 1), F32)
        for i in range(G):
            sink = jnp.where(head_of_row == i, sink_ref[kvh * G + i], sink)
        m = jnp.maximum(jnp.max(s, axis=-1, keepdims=True), sink)
        p = jnp.exp(s - m)
        den = jnp.sum(p, axis=-1, keepdims=True) + jnp.exp(sink - m)
        o = jnp.dot(p.astype(BF16), v, preferred_element_type=F32) / den
        outs += [o[i * BLOCK:(i + 1) * BLOCK, :] for i in range(G)]
    o_all = jnp.concatenate(outs, axis=1)
    g = g_ref[...].astype(F32)
    o_ref[...] = (o_all * (g / (1.0 + jnp.exp(-g)))).astype(o_ref.dtype)


def _swa(proj, sinks, band, *, B, S, q_blk, g_blk, k_blk, v_blk):
    M = B * S
    nb = S // BLOCK
    W, KW = SWA_WIDTH, SWA_KV_WIDTH

    def prev(b, n):
        return b * nb + jnp.maximum(n - 1, 0)

    vm = (2 * (3 * _nbytes((BLOCK, W), BF16) + 4 * _nbytes((BLOCK, KW), BF16))
          + 2 * _nbytes(band.shape, F32) + 16 * _nbytes((SWA_GROUP * BLOCK, 2 * BLOCK), F32))
    return pl.pallas_call(
        _swa_body,
        grid=(B, nb),
        in_specs=[pl.BlockSpec(memory_space=pltpu.SMEM),
                  pl.BlockSpec((BLOCK, W), lambda b, n: (b * nb + n, q_blk)),
                  pl.BlockSpec((BLOCK, KW), lambda b, n: (prev(b, n), k_blk)),
                  pl.BlockSpec((BLOCK, KW), lambda b, n: (b * nb + n, k_blk)),
                  pl.BlockSpec((BLOCK, KW), lambda b, n: (prev(b, n), v_blk)),
                  pl.BlockSpec((BLOCK, KW), lambda b, n: (b * nb + n, v_blk)),
                  pl.BlockSpec((BLOCK, W), lambda b, n: (b * nb + n, g_blk)),
                  pl.BlockSpec(band.shape, lambda b, n: (0, 0, 0))],
        out_specs=pl.BlockSpec((BLOCK, W), lambda b, n: (b * nb + n, 0)),
        out_shape=jax.ShapeDtypeStruct((M, W), BF16),
        compiler_params=_cparams(2, vm),
        name="swa_sink_attention",
    )(sinks.astype(F32), proj, proj, proj, proj, proj, proj, band)


def _odd_post_body(x_ref, t_ref, qg_ref, kvg_ref, lng_ref, lnb_ref, cq_ref, ckv_ref, kidx_ref,
                   w_ref):
    cq = x_ref[:, 0:DSA_Q_LATENT]
    cq_ref[...] = (cq * lax.rsqrt(jnp.mean(cq * cq, axis=-1, keepdims=True) + EPS)
                   * qg_ref[...]).astype(cq_ref.dtype)
    ckv = x_ref[:, DSA_Q_LATENT:DSA_Q_LATENT + DSA_KV_LATENT]
    ckv_ref[...] = (ckv * lax.rsqrt(jnp.mean(ckv * ckv, axis=-1, keepdims=True) + EPS)
                    * kvg_ref[...]).astype(ckv_ref.dtype)
    tail = t_ref[...]
    ki = tail[:, 0:IDX_DIM]
    mu = jnp.mean(ki, axis=-1, keepdims=True)
    xc = ki - mu
    var = jnp.mean(xc * xc, axis=-1, keepdims=True)
    kidx_ref[...] = (xc * lax.rsqrt(var + EPS) * lng_ref[...] + lnb_ref[...]).astype(kidx_ref.dtype)
    w_ref[...] = tail[:, IDX_DIM:IDX_DIM + IDX_HEADS] * (IDX_HEADS ** -0.5 * IDX_DIM ** -0.5)


def _odd_post(lat, tail, q_norm_g, kv_norm_g, ln_g, ln_b):
    M, n_lat = lat.shape
    tm = _pick_tile(M, 512)
    vm = 8 * _nbytes((tm, n_lat + LANES), F32)
    row = lambda i: (i, 0)
    fixed = lambda i: (0, 0)
    return pl.pallas_call(
        _odd_post_body,
        grid=(M // tm,),
        in_specs=[pl.BlockSpec((tm, n_lat), row),
                  pl.BlockSpec((tm, LANES), row),
                  pl.BlockSpec((1, DSA_Q_LATENT), fixed),
                  pl.BlockSpec((1, DSA_KV_LATENT), fixed),
                  pl.BlockSpec((1, IDX_DIM), fixed),
                  pl.BlockSpec((1, IDX_DIM), fixed)],
        out_specs=[pl.BlockSpec((tm, DSA_Q_LATENT), row),
                   pl.BlockSpec((tm, DSA_KV_LATENT), row),
                   pl.BlockSpec((tm, IDX_DIM), row),
                   pl.BlockSpec((tm, IDX_HEADS), row)],
        out_shape=[jax.ShapeDtypeStruct((M, DSA_Q_LATENT), BF16),
                   jax.ShapeDtypeStruct((M, DSA_KV_LATENT), BF16),
                   jax.ShapeDtypeStruct((M, IDX_DIM), BF16),
                   jax.ShapeDtypeStruct((M, IDX_HEADS), F32)],
        compiler_params=_cparams(1, vm),
        name="dsa_latent_norms",
    )(lat, tail, q_norm_g.reshape(1, -1).astype(F32), kv_norm_g.reshape(1, -1).astype(F32),
      ln_g.reshape(1, -1).astype(F32), ln_b.reshape(1, -1).astype(F32))


def _indexer_body(q_ref, k_ref, w_ref, o_ref, key_scr, *, tq, tk, S, topk):
    qi = pl.program_id(1)
    nck = (qi * tq) // tk + 1
    nt = tk // LANES
    rowpos = qi * tq + lax.broadcasted_iota(I32, (tq, tk), 0)
    colpos0 = lax.broadcasted_iota(I32, (tq, tk), 1)

    def score_chunk(c, _):
        off = pl.multiple_of(c * tk, tk)
        kc = k_ref[pl.ds(off, tk), :]
        acc = jnp.zeros((tq, tk), F32)
        for h in range(IDX_HEADS):
            qh = q_ref[:, h * IDX_DIM:(h + 1) * IDX_DIM]
            d = lax.dot_general(qh, kc, (((1,), (1,)), ((), ())), preferred_element_type=F32)
            acc = acc + jnp.maximum(d, 0.0) * w_ref[:, h:h + 1]
        bits = pltpu.bitcast(acc, I32)
        key = bits ^ ((bits >> 31) & 0x7FFFFFFF)
        key = jnp.where(colpos0 + off <= rowpos, key, INT_MIN)
        key_scr[:, pl.ds(off, tk)] = key
        return 0

    lax.fori_loop(0, nck, score_chunk, 0)

    def count(pred):
        def body(c, acc):
            off = pl.multiple_of(c * tk, tk)
            kk = key_scr[:, pl.ds(off, tk)]
            for t in range(nt):
                acc = acc + jnp.where(pred(kk[:, t * LANES:(t + 1) * LANES]), 1, 0)
            return acc
        part = lax.fori_loop(0, nck, body, jnp.zeros((tq, LANES), I32))
        return jnp.sum(part, axis=-1, keepdims=True)

    def bisect(i, t_u):
        cand_u = t_u | lax.shift_left(jnp.int32(1), 31 - i)
        cand = cand_u ^ INT_MIN
        cnt = count(lambda kk: kk >= cand)
        return jnp.where(cnt >= topk, cand_u, t_u)

    t_u = lax.fori_loop(0, 32, bisect, jnp.zeros((tq, 1), I32))
    thr = t_u ^ INT_MIN
    has_thr = thr > INT_MIN
    thr_sel = jnp.maximum(thr, INT_MIN + 1)
    n_ge = count(lambda kk: kk >= thr)
    tie = jnp.max(jnp.where(has_thr & (n_ge > topk), 1, 0)) > 0

    def write_rest():
        def body(c, _):
            off = pl.multiple_of(c * tk, tk)
            o_ref[:, pl.ds(off, tk)] = jnp.full((tq, tk), NEG, o_ref.dtype)
            return 0
        lax.fori_loop(nck, S // tk, body, 0)

    @pl.when(jnp.logical_not(tie))
    def _():
        def body(c, _):
            off = pl.multiple_of(c * tk, tk)
            kk = key_scr[:, pl.ds(off, tk)]
            o_ref[:, pl.ds(off, tk)] = jnp.where(kk >= thr_sel, 0.0, NEG).astype(o_ref.dtype)
            return 0
        lax.fori_loop(0, nck, body, 0)

    @pl.when(tie)
    def _():
        n_gt = count(lambda kk: kk > thr)
        room = jnp.where(has_thr, topk - n_gt, 0).astype(F32)
        r = lax.broadcasted_iota(I32, (tk, tk), 0)
        cc = lax.broadcasted_iota(I32, (tk, tk), 1)
        upper = (r <= cc).astype(BF16)

        def body(c, run):
            off = pl.multiple_of(c * tk, tk)
            kk = key_scr[:, pl.ds(off, tk)]
            eq = kk == thr
            eq_f = jnp.where(eq, 1.0, 0.0)
            rank = jnp.dot(eq_f.astype(BF16), upper, preferred_element_type=F32) + run
            sel = (kk > thr) | (eq & (rank <= room))
            o_ref[:, pl.ds(off, tk)] = jnp.where(sel, 0.0, NEG).astype(o_ref.dtype)
            return run + jnp.sum(eq_f, axis=-1, keepdims=True)
        lax.fori_loop(0, nck, body, jnp.zeros((tq, 1), F32))

    write_rest()


def _indexer(qq, kidx, widx, *, B, S, topk, q_blk):
    M = B * S
    tq = BLOCK
    tk = _pick_tile(S, 512)
    nq = S // tq
    vm = (2 * (_nbytes((tq, IDX_WIDTH), BF16) + _nbytes((S, LANES), BF16) + _nbytes((tq, LANES), F32)
               + _nbytes((tq, S), BF16)) + _nbytes((tq, S), I32) + 16 * _nbytes((tq, tk), F32))
    return pl.pallas_call(
        functools.partial(_indexer_body, tq=tq, tk=tk, S=S, topk=topk),
        grid=(B, nq),
        in_specs=[pl.BlockSpec((tq, IDX_WIDTH), lambda b, i: (b * nq + i, q_blk)),
                  pl.BlockSpec((S, IDX_DIM), lambda b, i: (b, 0)),
                  pl.BlockSpec((tq, IDX_HEADS), lambda b, i: (b * nq + i, 0))],
        out_specs=pl.BlockSpec((tq, S), lambda b, i: (b * nq + i, 0)),
        out_shape=jax.ShapeDtypeStruct((M, S), BF16),
        scratch_shapes=[pltpu.VMEM((tq, S), I32)],
        compiler_params=_cparams(2, vm),
        name="dsa_indexer_topk",
    )(qq, kidx, widx)


def _t5_bucket(rel):
    n = jnp.maximum(rel, 0)
    max_exact = T5_BUCKETS // 2
    nf = jnp.maximum(n, 1).astype(F32)
    large = max_exact + (jnp.log(nf / max_exact) / math.log(T5_MAX_DIST / max_exact)
                         * (T5_BUCKETS - max_exact)).astype(I32)
    large = jnp.minimum(large, T5_BUCKETS - 1)
    return jnp.where(n < max_exact, n, large)


def _band_tables(t5_table):
    rel = jnp.arange(BLOCK)[:, None] + BLOCK - jnp.arange(2 * BLOCK)[None, :]
    band = t5_table.astype(F32)[_t5_bucket(rel)].transpose(2, 0, 1)
    swa = band[:SWA_Q_HEADS]
    far = t5_table.astype(F32)[T5_BUCKETS - 1, SWA_Q_HEADS:]
    dsa = (band[SWA_Q_HEADS:] - far[:, None, None]) * LOG2E
    return swa, dsa


def _flash_tile(S):
    return _pick_tile(S, 512)


def _even_layer(h, hn, w_in_t, b_f, sinks, w_out, layer, band_swa, B, S):
    tn = PROJ_TILE
    f_col = 3 * FOX_WIDTH
    n_main = 4 * FOX_WIDTH + 2 * SWA_WIDTH + 2 * SWA_KV_WIDTH
    assert f_col % tn == 0 and FOX_WIDTH % tn == 0 and SWA_WIDTH % tn == 0 and 2 * SWA_KV_WIDTH == tn
    blk = lambda cols: cols // tn
    qb_lo = blk(4 * FOX_WIDTH)
    kv_blk = blk(4 * FOX_WIDTH + SWA_WIDTH)
    last = blk(n_main) - 1

    def out_block(j):
        return jnp.where(j == kv_blk, last, jnp.where(j > kv_blk, j - 1, j))

    w_main = _Weight(w_in_t, layer, transposed=True, skip=FOX_HEADS, skip_from=blk(f_col),
                     scales=((0, blk(FOX_WIDTH), FOX_HEAD_DIM ** -0.5 * LOG2E),
                             (qb_lo, qb_lo + blk(SWA_WIDTH), SWA_HEAD_DIM ** -0.5)))
    proj = _matmul([hn], [w_main], n_main, BF16, tn=tn, out_block=out_block, name="even_in_proj")
    f = _matmul([hn], [_Weight(w_in_t, layer, transposed=True, col0=f_col)], LANES, F32,
                tn=LANES, name="even_gate_proj")
    decay = _fox_cum(f, b_f, B, S)
    nh = FOX_HEADS
    ya = _flash(proj, proj, proj, proj, B=B, S=S, H=nh, q_blk=0, k_blk=nh, v_blk=2 * nh,
                g_blk=3 * nh, hb=4, T=_flash_tile(S), decay=decay)
    base = 4 * FOX_WIDTH
    yb = _swa(proj, sinks, band_swa, B=B, S=S,
              q_blk=base // SWA_WIDTH, g_blk=base // SWA_WIDTH + 1,
              k_blk=(base + 2 * SWA_WIDTH) // SWA_KV_WIDTH,
              v_blk=(base + 2 * SWA_WIDTH) // SWA_KV_WIDTH + 1)
    return _matmul([ya, yb], [_Weight(w_out, layer), _Weight(w_out, layer, row0=FOX_WIDTH)],
                   h.shape[1], F32, residual=h, tn=tn, name="even_out_proj")


def _odd_layer(h, hn, w_in_t, q_norm_g, kv_norm_g, w_uq, w_uq_idx, ln_g, ln_b, w_uk, w_uv, w_out,
               layer, band_dsa, B, S, topk):
    tn = PROJ_TILE
    n_lat = DSA_Q_LATENT + DSA_KV_LATENT
    n_small = n_lat + IDX_DIM + IDX_HEADS
    assert n_lat % tn == 0 and n_small - n_lat <= LANES
    win = functools.partial(_Weight, w_in_t, layer, transposed=True)
    lat = _matmul([hn], [win()], n_lat, F32, tn=tn, name="odd_latent_proj")
    tail = _matmul([hn], [win(col0=n_lat)], LANES, F32, tn=LANES, name="odd_index_proj")
    gc = _matmul([hn], [win(col0=n_small)], DSA_WIDTH, BF16, tn=tn, name="odd_gate_proj")
    cq, ckv, kidx, widx = _odd_post(lat, tail, q_norm_g, kv_norm_g, ln_g, ln_b)
    tq = 2 * tn
    w_q = _Weight(w_uq, layer, scales=((0, DSA_WIDTH // tq, DSA_QK_DIM ** -0.5 * LOG2E),))
    q = _matmul([cq], [w_q], DSA_WIDTH, BF16, tn=tq, name="odd_q_up")
    q_idx = _matmul([cq], [_Weight(w_uq_idx, layer)], IDX_WIDTH, BF16, tn=tq, name="odd_qidx_up")
    n_odd, L = w_uk.shape[:2]
    k = _matmul([ckv], [_Weight(w_uk.reshape(n_odd, L, DSA_WIDTH), layer)], DSA_WIDTH, BF16,
                tn=tq, name="odd_k_up")
    v = _matmul([ckv], [_Weight(w_uv.reshape(n_odd, L, DSA_WIDTH), layer)], DSA_WIDTH, BF16,
                tn=tq, name="odd_v_up")
    mask_bias = _indexer(q_idx, kidx, widx, B=B, S=S, topk=topk, q_blk=0)
    y = _flash(q, k, v, gc, B=B, S=S, H=DSA_HEADS, q_blk=0, k_blk=0, v_blk=0, g_blk=0, hb=2,
               T=_flash_tile(S), mask_bias=mask_bias, band=band_dsa)
    return _matmul([y], [_Weight(w_out, layer)], h.shape[1], F32, residual=h, tn=tn,
                   name="odd_out_proj")


def kernel(x, p, t5_table, norm_g, even_w_in, even_b_f, even_sinks, even_w_out, odd_w_in, odd_q_norm_g, odd_kv_norm_g, odd_w_uq, odd_w_uq_idx, odd_idx_ln_g, odd_idx_ln_b, odd_w_uk, odd_w_uv, odd_w_out, ple_w_proj, ple_norm_g, ple_w_gate, final_g):
    B, S, D = x.shape
    depth = norm_g.shape[0]
    M = B * S
    topk = min(TOPK_MAX, S // 4)
    band_swa, band_dsa = _band_tables(t5_table)
    h = x.reshape(M, D).astype(F32)
    p_all = p.reshape(depth, M, p.shape[-1]).astype(F32)
    even_w_in_t = jnp.swapaxes(even_w_in, 1, 2)
    odd_w_in_t = jnp.swapaxes(odd_w_in, 1, 2)
    for i in range(depth):
        j = i // 2
        hn = _rmsnorm(h, norm_g[i], BF16)
        if i % 2 == 0:
            h = _even_layer(h, hn, even_w_in_t, even_b_f[j], even_sinks[j], even_w_out, j,
                            band_swa, B, S)
        else:
            h = _odd_layer(h, hn, odd_w_in_t, odd_q_norm_g[j], odd_kv_norm_g[j], odd_w_uq,
                           odd_w_uq_idx, odd_idx_ln_g[j], odd_idx_ln_b[j], odd_w_uk,
                           odd_w_uv, odd_w_out, j, band_dsa, B, S, topk)
        hn = _rmsnorm(h, ple_norm_g[i], BF16)
        h = _ple(hn, ple_w_gate, p_all, i, ple_w_proj, h)
    return _rmsnorm(h, final_g, F32).reshape(B, S, D)
```

```python
import functools
import math
from typing import NamedTuple

import jax
import jax.numpy as jnp
from jax import lax
from jax.experimental import pallas as pl
from jax.experimental.pallas import tpu as pltpu

F32 = jnp.float32
BF16 = jnp.bfloat16
I32 = jnp.int32

EPS = 1e-6
BLOCK = 128
PLE_DIM = 256
FOX_HEADS = 16
FOX_HEAD_DIM = 128
FOX_WIDTH = FOX_HEADS * FOX_HEAD_DIM
SWA_Q_HEADS = 32
SWA_KV_HEADS = 4
SWA_GROUP = SWA_Q_HEADS // SWA_KV_HEADS
SWA_HEAD_DIM = 64
SWA_WIDTH = SWA_Q_HEADS * SWA_HEAD_DIM
SWA_KV_WIDTH = SWA_KV_HEADS * SWA_HEAD_DIM
DSA_HEADS = 32
DSA_QK_DIM = 128
DSA_V_DIM = 128
DSA_WIDTH = DSA_HEADS * DSA_V_DIM
DSA_Q_LATENT = 1024
DSA_KV_LATENT = 512
IDX_HEADS = 32
IDX_DIM = 64
IDX_WIDTH = IDX_HEADS * IDX_DIM
TOPK_MAX = 256
T5_BUCKETS = 32
T5_MAX_DIST = 128

LANES = 128
SUBLANES = 8
V7X_VMEM_BYTES = 64 * 1024 * 1024
PROJ_TILE = 512
NEG = -1e30
INT_MIN = -2147483648
LOG2E = math.log2(math.e)


def _cparams(n_axes, vmem_bytes):
    limit = min(int(vmem_bytes * 1.25) + (4 << 20), V7X_VMEM_BYTES - (8 << 20))
    return pltpu.CompilerParams(dimension_semantics=("arbitrary",) * n_axes,
                                vmem_limit_bytes=limit)


def _pick_tile(n, cap):
    if n <= cap:
        return n
    best = None
    for d in range(LANES, cap + 1, LANES):
        if n % d == 0:
            best = d
    assert best is not None, (n, cap)
    return best


def _nbytes(shape, dtype):
    n = 1
    for s in shape:
        n *= s
    return n * jnp.dtype(dtype).itemsize


def _rmsnorm_body(x_ref, g_ref, o_ref):
    x = x_ref[...]
    ms = jnp.mean(x * x, axis=-1, keepdims=True)
    o_ref[...] = (x * lax.rsqrt(ms + EPS) * g_ref[...]).astype(o_ref.dtype)


def _rmsnorm(x, g, out_dtype):
    M, D = x.shape
    tm = _pick_tile(M, 256)
    vm = 2 * (_nbytes((tm, D), F32) + _nbytes((tm, D), out_dtype)) + 3 * _nbytes((tm, D), F32)
    return pl.pallas_call(
        _rmsnorm_body,
        grid=(M // tm,),
        in_specs=[pl.BlockSpec((tm, D), lambda i: (i, 0)),
                  pl.BlockSpec((1, D), lambda i: (0, 0))],
        out_specs=pl.BlockSpec((tm, D), lambda i: (i, 0)),
        out_shape=jax.ShapeDtypeStruct((M, D), out_dtype),
        compiler_params=_cparams(1, vm),
        name="rmsnorm",
    )(x, g.reshape(1, D).astype(F32))


class _Weight(NamedTuple):
    w: jax.Array
    layer: int
    row0: int = 0
    col0: int = 0
    transposed: bool = False
    skip: int = 0
    skip_from: int = 0
    scales: tuple = ()


def _matmul_body(*refs, wts, has_res):
    n = len(wts)
    x_refs, w_refs = refs[:n], refs[n:2 * n]
    r_ref = refs[2 * n] if has_res else None
    o_ref = refs[2 * n + (1 if has_res else 0)]
    w_scrs = refs[2 * n + (2 if has_res else 1):]
    j = pl.program_id(0)

    @pl.when(pl.program_id(1) == 0)
    def _():
        for wt, w_ref, w_scr in zip(wts, w_refs, w_scrs):
            w = w_ref[0] if wt.transposed else w_ref[...]
            scale = None
            for lo, hi, val in wt.scales:
                scale = jnp.where((j >= lo) & (j < hi), val, 1.0 if scale is None else scale)
            if scale is not None:
                w = w * scale
            w_scr[...] = w.astype(BF16)

    acc = None
    for wt, x_ref, w_scr in zip(wts, x_refs, w_scrs):
        contract = (((1,), (1,)), ((), ())) if wt.transposed else (((1,), (0,)), ((), ()))
        d = lax.dot_general(x_ref[...], w_scr[...], contract, preferred_element_type=F32)
        acc = d if acc is None else acc + d
    if has_res:
        acc = acc + r_ref[...]
    o_ref[...] = acc.astype(o_ref.dtype)


def _weight_spec(wt, K, tn):
    layer = wt.layer
    if wt.transposed:
        assert wt.row0 == 0 and wt.col0 % SUBLANES == 0 and wt.skip % SUBLANES == 0

        def start(j):
            skip = jnp.where(j >= wt.skip_from, wt.skip, 0) if wt.skip else 0
            return pl.multiple_of(wt.col0 + j * tn + skip, SUBLANES)
        dims = (pl.Element(1), pl.Element(tn), pl.Element(K))
        return pl.BlockSpec(dims, lambda j, i: (layer, start(j), 0)), (tn, K)
    assert wt.row0 % K == 0 and wt.col0 % tn == 0 and tn % LANES == 0 and wt.skip == 0
    rb, cb = wt.row0 // K, wt.col0 // tn
    return pl.BlockSpec((None, K, tn), lambda j, i: (layer, rb, cb + j)), (K, tn)


def _matmul(xs, wts, n_out, out_dtype, *, residual=None, tm_cap=1024, tn=512, out_block=None,
            name="matmul"):
    M = xs[0].shape[0]
    tm = _pick_tile(M, tm_cap)
    tn = min(tn, n_out)
    assert n_out % tn == 0
    out_block = out_block or (lambda j: j)
    in_specs, args, vm, scratch = [], [], 0, []
    for x in xs:
        K = x.shape[1]
        in_specs.append(pl.BlockSpec((tm, K), lambda j, i: (i, 0)))
        args.append(x)
        vm += 2 * _nbytes((tm, K), x.dtype)
    for x, wt in zip(xs, wts):
        spec, tile = _weight_spec(wt, x.shape[1], tn)
        in_specs.append(spec)
        args.append(wt.w)
        vm += 2 * _nbytes(tile, F32) + _nbytes(tile, BF16)
        scratch.append(pltpu.VMEM(tile, BF16))
    if residual is not None:
        in_specs.append(pl.BlockSpec((tm, tn), lambda j, i: (i, out_block(j))))
        args.append(residual)
        vm += 2 * _nbytes((tm, tn), F32)
    vm += 2 * _nbytes((tm, tn), out_dtype) + 2 * _nbytes((tm, tn), F32)
    return pl.pallas_call(
        functools.partial(_matmul_body, wts=tuple(wt._replace(w=None) for wt in wts),
                          has_res=residual is not None),
        grid=(n_out // tn, M // tm),
        in_specs=in_specs,
        out_specs=pl.BlockSpec((tm, tn), lambda j, i: (i, out_block(j))),
        out_shape=jax.ShapeDtypeStruct((M, n_out), out_dtype),
        scratch_shapes=scratch,
        compiler_params=_cparams(2, vm),
        name=name,
    )(*args)


def _ple_body(hn_ref, wg_ref, p_ref, wp_ref, h_ref, o_ref, wg_scr, wp_scr):
    @pl.when(pl.program_id(1) == 0)
    def _():
        wg_scr[...] = wg_ref[...].astype(BF16)
        wp_scr[...] = wp_ref[...].astype(BF16)

    z = jnp.dot(hn_ref[...], wg_scr[...], preferred_element_type=F32)
    gate = 1.0 / (1.0 + jnp.exp(-z))
    pe = jnp.dot(p_ref[...].astype(BF16), wp_scr[...], preferred_element_type=F32)
    o_ref[...] = h_ref[...] + gate * pe


def _ple(hn, wg_all, p_all, layer, wp_all, h):
    M, D = h.shape
    P = p_all.shape[-1]
    tm = _pick_tile(M, 1024)
    tn = _pick_tile(D, PROJ_TILE)
    vm = (2 * (_nbytes((tm, D), BF16) + _nbytes((D, tn), F32) + _nbytes((tm, P), F32)
               + _nbytes((P, tn), F32) + 2 * _nbytes((tm, tn), F32))
          + _nbytes((D + P, tn), BF16) + 3 * _nbytes((tm, tn), F32))
    return pl.pallas_call(
        _ple_body,
        grid=(D // tn, M // tm),
        in_specs=[pl.BlockSpec((tm, D), lambda j, i: (i, 0)),
                  pl.BlockSpec((None, D, tn), lambda j, i: (layer, 0, j)),
                  pl.BlockSpec((None, tm, P), lambda j, i: (layer, i, 0)),
                  pl.BlockSpec((None, P, tn), lambda j, i: (layer, 0, j)),
                  pl.BlockSpec((tm, tn), lambda j, i: (i, j))],
        out_specs=pl.BlockSpec((tm, tn), lambda j, i: (i, j)),
        out_shape=jax.ShapeDtypeStruct((M, D), F32),
        scratch_shapes=[pltpu.VMEM((D, tn), BF16), pltpu.VMEM((P, tn), BF16)],
        compiler_params=_cparams(2, vm),
        name="ple_gate",
    )(hn, wg_all, p_all, wp_all, h)


def _cum_body(f_ref, b_ref, o_ref, scr, *, S):
    x = f_ref[...] + b_ref[...]
    neg_log_f = jnp.maximum(-x, 0.0) + jnp.log1p(jnp.exp(-jnp.abs(x)))
    r = lax.broadcasted_iota(I32, (BLOCK, BLOCK), 0)
    c = lax.broadcasted_iota(I32, (BLOCK, BLOCK), 1)
    tri = (c <= r).astype(BF16)
    run = jnp.zeros((1, LANES), F32)
    for i in range(S // BLOCK):
        blk = neg_log_f[i * BLOCK:(i + 1) * BLOCK, :]
        hi = blk.astype(BF16)
        r1 = blk - hi.astype(F32)
        mid = r1.astype(BF16)
        lo = (r1 - mid.astype(F32)).astype(BF16)
        cs = (jnp.dot(tri, hi, preferred_element_type=F32)
              + jnp.dot(tri, mid, preferred_element_type=F32)
              + jnp.dot(tri, lo, preferred_element_type=F32)) + run
        scr[i * BLOCK:(i + 1) * BLOCK, :] = cs
        run = cs[BLOCK - 1:BLOCK, :]
    o_ref[0] = (scr[...] * LOG2E).T[:FOX_HEADS, :]


def _fox_cum(f, b_f, B, S):
    b_pad = jnp.zeros((1, LANES), F32).at[0, :FOX_HEADS].set(b_f.astype(F32))
    vm = 6 * _nbytes((S, LANES), F32)
    return pl.pallas_call(
        functools.partial(_cum_body, S=S),
        grid=(B,),
        in_specs=[pl.BlockSpec((S, LANES), lambda b: (b, 0)),
                  pl.BlockSpec((1, LANES), lambda b: (0, 0))],
        out_specs=pl.BlockSpec((1, FOX_HEADS, S), lambda b: (b, 0, 0)),
        out_shape=jax.ShapeDtypeStruct((B, FOX_HEADS, S), F32),
        scratch_shapes=[pltpu.VMEM((S, LANES), F32)],
        compiler_params=_cparams(1, vm),
        name="fox_decay_cumsum",
    )(f, b_pad)


def _flash_body(*refs, T, hb, fox):
    if fox:
        q_ref, k_ref, v_ref, g_ref, c_ref, o_ref = refs[:6]
    else:
        q_ref, k_ref, v_ref, g_ref, mb_ref, d_ref, o_ref, bias_scr, mask_scr = refs[:9]
    s0_scr, s1_scr, m_scr, l_scr, acc_scr = refs[-5:]
    qi = pl.program_id(1)
    hd = LANES
    head_cols = [slice(hh * hd, (hh + 1) * hd) for hh in range(hb)]

    if not fox:
        hg = pl.program_id(2)

        @pl.when((pl.program_id(0) == 0) & (qi == 0) & (hg == 0))
        def _():
            bias_scr[...] = jnp.zeros(bias_scr.shape, F32)

        @pl.when(hg == 0)
        def _():
            def widen(c, _):
                off = pl.multiple_of(c * T, T)
                mask_scr[:, pl.ds(off, T)] = mb_ref[:, pl.ds(off, T)].astype(F32)
                return 0
            lax.fori_loop(0, qi + 1, widen, 0)

        for hh in range(hb):
            for a in range(T // BLOCK):
                lo = T + (a - 1) * BLOCK
                bias_scr[hh, a * BLOCK:(a + 1) * BLOCK, lo:lo + 2 * BLOCK] = d_ref[hh]

    qs = [q_ref[:, cols] for cols in head_cols]
    m_scr[...] = jnp.full(m_scr.shape, NEG, F32)
    l_scr[...] = jnp.zeros(l_scr.shape, F32)
    acc_scr[...] = jnp.zeros(acc_scr.shape, F32)

    def scores(c, buf, kind):
        off = pl.multiple_of(c * T, T)
        for hh, cols in enumerate(head_cols):
            s = lax.dot_general(qs[hh], k_ref[pl.ds(off, T), cols],
                                (((1,), (1,)), ((), ())), preferred_element_type=F32)
            if fox:
                s = s + c_ref[0, hh:hh + 1, pl.ds(off, T)]
                if kind == "diag":
                    row = lax.broadcasted_iota(I32, (T, T), 0)
                    col = lax.broadcasted_iota(I32, (T, T), 1)
                    s = jnp.where(row >= col, s, NEG)
            else:
                s = s + mask_scr[:, pl.ds(off, T)]
                if kind == "prev":
                    s = s + bias_scr[hh, :, 0:T]
                elif kind == "diag":
                    s = s + bias_scr[hh, :, T:2 * T]
            buf[hh] = s

    def absorb(c, buf):
        off = pl.multiple_of(c * T, T)
        for hh, cols in enumerate(head_cols):
            m_old = m_scr[hh]
            m_new = jnp.maximum(m_old, jnp.max(buf[hh], axis=-1, keepdims=True))
            m_scr[hh] = m_new
            alpha = jnp.exp2(m_old - m_new)
            p = jnp.exp2(buf[hh] - m_new)
            l_scr[hh] = alpha * l_scr[hh] + jnp.sum(p, axis=-1, keepdims=True)
            v = v_ref[pl.ds(off, T), cols]
            acc_scr[hh] = alpha * acc_scr[hh] + jnp.dot(p.astype(BF16), v,
                                                        preferred_element_type=F32)

    bufs = (s0_scr, s1_scr)
    n_plain = qi if fox else jnp.maximum(qi - 1, 0)
    n_pairs = jnp.maximum(n_plain - 1, 0) // 2
    pl.when(n_plain > 0)(lambda: scores(0, s0_scr, "plain"))

    def pair(u, _):
        c = 2 * u
        scores(c + 1, s1_scr, "plain")
        absorb(c, s0_scr)
        scores(c + 2, s0_scr, "plain")
        absorb(c + 1, s1_scr)
        return 0

    lax.fori_loop(0, n_pairs, pair, 0)
    c0 = 2 * n_pairs
    left = n_plain - c0

    def drain(kinds, primed):
        if not primed:
            scores(c0, bufs[0], kinds[0])
        for t, kind in enumerate(kinds):
            if t + 1 < len(kinds):
                scores(c0 + t + 1, bufs[(t + 1) % 2], kinds[t + 1])
            absorb(c0 + t, bufs[t % 2])

    special = ["diag"] if fox else ["prev", "diag"]
    pl.when(qi == 0)(lambda: drain(["diag"], False))
    if not fox:
        pl.when(qi == 1)(lambda: drain(["prev", "diag"], False))
    pl.when(left == 1)(lambda: drain(["plain"] + special, True))
    pl.when(left == 2)(lambda: drain(["plain", "plain"] + special, True))

    for hh, cols in enumerate(head_cols):
        g = g_ref[:, cols].astype(F32)
        o_ref[:, cols] = (acc_scr[hh] / l_scr[hh] * (g / (1.0 + jnp.exp(-g)))).astype(o_ref.dtype)


def _flash(q_arr, k_arr, v_arr, g_arr, *, B, S, H, q_blk, k_blk, v_blk, g_blk, hb, T,
           decay=None, mask_bias=None, band=None):
    fox = decay is not None
    M = B * S
    nq = S // T
    w = hb * LANES
    qb, kb, vb, gb = q_blk // hb, k_blk // hb, v_blk // hb, g_blk // hb
    in_specs = [pl.BlockSpec((T, w), lambda b, i, h: (b * nq + i, qb + h)),
                pl.BlockSpec((S, w), lambda b, i, h: (b, kb + h)),
                pl.BlockSpec((S, w), lambda b, i, h: (b, vb + h)),
                pl.BlockSpec((T, w), lambda b, i, h: (b * nq + i, gb + h))]
    args = [q_arr, k_arr, v_arr, g_arr]
    vm = (2 * (3 * _nbytes((T, w), BF16) + 2 * _nbytes((S, w), BF16))
          + 6 * hb * _nbytes((T, T), F32))
    scratch = []
    if fox:
        in_specs.append(pl.BlockSpec((1, hb, S), lambda b, i, h: (b * (H // hb) + h, 0, 0)))
        args.append(decay.reshape(B * H // hb, hb, S))
        vm += 2 * _nbytes((8, S), F32)
    else:
        in_specs.append(pl.BlockSpec((T, S), lambda b, i, h: (b * nq + i, 0)))
        in_specs.append(pl.BlockSpec((hb, BLOCK, 2 * BLOCK), lambda b, i, h: (h, 0, 0)))
        args += [mask_bias, band]
        scratch = [pltpu.VMEM((hb, T, 2 * T), F32), pltpu.VMEM((T, S), F32)]
        vm += (2 * _nbytes((T, S), BF16) + _nbytes((hb, T, 2 * T), F32) + _nbytes((T, S), F32)
               + 2 * _nbytes((hb, BLOCK, 2 * BLOCK), F32))
    scratch += [pltpu.VMEM((hb, T, T), F32), pltpu.VMEM((hb, T, T), F32),
                pltpu.VMEM((hb, T, 1), F32), pltpu.VMEM((hb, T, 1), F32),
                pltpu.VMEM((hb, T, LANES), F32)]
    vm += hb * (2 * _nbytes((T, T), F32) + 3 * _nbytes((T, LANES), F32))
    return pl.pallas_call(
        functools.partial(_flash_body, T=T, hb=hb, fox=fox),
        grid=(B, nq, H // hb),
        in_specs=in_specs,
        out_specs=pl.BlockSpec((T, w), lambda b, i, h: (b * nq + i, h)),
        out_shape=jax.ShapeDtypeStruct((M, H * LANES), BF16),
        scratch_shapes=scratch,
        compiler_params=_cparams(3, vm),
        name="fox_attention" if fox else "dsa_attention",
    )(*args)


def _swa_body(sink_ref, q_ref, kp_ref, kc_ref, vp_ref, vc_ref, g_ref, bias_ref, o_ref):
    G, dh = SWA_GROUP, SWA_HEAD_DIM
    rows = G * BLOCK
    outs = []
    for kvh in range(SWA_KV_HEADS):
        kcols = slice(kvh * dh, (kvh + 1) * dh)
        k = jnp.concatenate([kp_ref[:, kcols], kc_ref[:, kcols]], axis=0)
        v = jnp.concatenate([vp_ref[:, kcols], vc_ref[:, kcols]], axis=0)
        qs = jnp.concatenate(
            [q_ref[:, (kvh * G + i) * dh:(kvh * G + i + 1) * dh] for i in range(G)], axis=0)
        s = lax.dot_general(qs, k, (((1,), (1,)), ((), ())), preferred_element_type=F32)
        s = s + bias_ref[kvh * G:(kvh + 1) * G].reshape(rows, 2 * BLOCK)
        sink = sink_ref[kvh * rows:(kvh + 1) * rows, :]
        m = jnp.maximum(jnp.max(s, axis=-1, keepdims=True), sink)
        p = jnp.exp2(s - jnp.concatenate([m, m], axis=1))
        den = jnp.sum(p, axis=-1, keepdims=True) + jnp.exp2(sink - m)
        o = jnp.dot(p.astype(BF16), v, preferred_element_type=F32) / den[:, :dh]
        outs += [o[i * BLOCK:(i + 1) * BLOCK, :] for i in range(G)]
    o_all = jnp.concatenate(outs, axis=1)
    g = g_ref[...].astype(F32)
    o_ref[...] = (o_all * (g / (1.0 + jnp.exp(-g)))).astype(o_ref.dtype)


def _swa(proj, sinks, band, *, B, S, q_blk, g_blk, k_blk, v_blk):
    M = B * S
    nb = S // BLOCK
    W, KW = SWA_WIDTH, SWA_KV_WIDTH

    def prev(b, n):
        return b * nb + jnp.maximum(n - 1, 0)

    sink_col = jnp.broadcast_to(jnp.repeat(sinks.astype(F32) * LOG2E, BLOCK)[:, None],
                                (SWA_Q_HEADS * BLOCK, LANES))
    vm = (2 * (3 * _nbytes((BLOCK, W), BF16) + 4 * _nbytes((BLOCK, KW), BF16))
          + 2 * _nbytes((SWA_Q_HEADS * BLOCK, LANES), F32)
          + 2 * _nbytes(band.shape[1:], F32) + 16 * _nbytes((SWA_GROUP * BLOCK, 2 * BLOCK), F32))
    return pl.pallas_call(
        _swa_body,
        grid=(B, nb),
        in_specs=[pl.BlockSpec((SWA_Q_HEADS * BLOCK, LANES), lambda b, n: (0, 0)),
                  pl.BlockSpec((BLOCK, W), lambda b, n: (b * nb + n, q_blk)),
                  pl.BlockSpec((BLOCK, KW), lambda b, n: (prev(b, n), k_blk)),
                  pl.BlockSpec((BLOCK, KW), lambda b, n: (b * nb + n, k_blk)),
                  pl.BlockSpec((BLOCK, KW), lambda b, n: (prev(b, n), v_blk)),
                  pl.BlockSpec((BLOCK, KW), lambda b, n: (b * nb + n, v_blk)),
                  pl.BlockSpec((BLOCK, W), lambda b, n: (b * nb + n, g_blk)),
                  pl.BlockSpec((None,) + band.shape[1:],
                               lambda b, n: (jnp.minimum(n, 1), 0, 0, 0))],
        out_specs=pl.BlockSpec((BLOCK, W), lambda b, n: (b * nb + n, 0)),
        out_shape=jax.ShapeDtypeStruct((M, W), BF16),
        compiler_params=_cparams(2, vm),
        name="swa_sink_attention",
    )(sink_col, proj, proj, proj, proj, proj, proj, band)


def _odd_post_body(x_ref, t_ref, qg_ref, kvg_ref, lng_ref, lnb_ref, cq_ref, ckv_ref, kidx_ref,
                   w_ref):
    cq = x_ref[:, 0:DSA_Q_LATENT]
    cq_ref[...] = (cq * lax.rsqrt(jnp.mean(cq * cq, axis=-1, keepdims=True) + EPS)
                   * qg_ref[...]).astype(cq_ref.dtype)
    ckv = x_ref[:, DSA_Q_LATENT:DSA_Q_LATENT + DSA_KV_LATENT]
    ckv_ref[...] = (ckv * lax.rsqrt(jnp.mean(ckv * ckv, axis=-1, keepdims=True) + EPS)
                    * kvg_ref[...]).astype(ckv_ref.dtype)
    tail = t_ref[...]
    ki = tail[:, 0:IDX_DIM]
    mu = jnp.mean(ki, axis=-1, keepdims=True)
    xc = ki - mu
    var = jnp.mean(xc * xc, axis=-1, keepdims=True)
    kidx_ref[...] = (xc * lax.rsqrt(var + EPS) * lng_ref[...] + lnb_ref[...]).astype(kidx_ref.dtype)
    w_ref[...] = tail[:, IDX_DIM:IDX_DIM + IDX_HEADS] * (IDX_HEADS ** -0.5 * IDX_DIM ** -0.5)


def _odd_post(lat, tail, q_norm_g, kv_norm_g, ln_g, ln_b):
    M, n_lat = lat.shape
    tm = _pick_tile(M, 512)
    vm = 8 * _nbytes((tm, n_lat + LANES), F32)
    row = lambda i: (i, 0)
    fixed = lambda i: (0, 0)
    return pl.pallas_call(
        _odd_post_body,
        grid=(M // tm,),
        in_specs=[pl.BlockSpec((tm, n_lat), row),
                  pl.BlockSpec((tm, LANES), row),
                  pl.BlockSpec((1, DSA_Q_LATENT), fixed),
                  pl.BlockSpec((1, DSA_KV_LATENT), fixed),
                  pl.BlockSpec((1, IDX_DIM), fixed),
                  pl.BlockSpec((1, IDX_DIM), fixed)],
        out_specs=[pl.BlockSpec((tm, DSA_Q_LATENT), row),
                   pl.BlockSpec((tm, DSA_KV_LATENT), row),
                   pl.BlockSpec((tm, IDX_DIM), row),
                   pl.BlockSpec((tm, IDX_HEADS), row)],
        out_shape=[jax.ShapeDtypeStruct((M, DSA_Q_LATENT), BF16),
                   jax.ShapeDtypeStruct((M, DSA_KV_LATENT), BF16),
                   jax.ShapeDtypeStruct((M, IDX_DIM), BF16),
                   jax.ShapeDtypeStruct((M, IDX_HEADS), F32)],
        compiler_params=_cparams(1, vm),
        name="dsa_latent_norms",
    )(lat, tail, q_norm_g.reshape(1, -1).astype(F32), kv_norm_g.reshape(1, -1).astype(F32),
      ln_g.reshape(1, -1).astype(F32), ln_b.reshape(1, -1).astype(F32))


def _indexer_body(q_ref, k_ref, w_ref, o_ref, key_scr, *, tq, tk, S, topk):
    qi = pl.program_id(1)
    nck = (qi * tq) // tk + 1
    nt = tk // LANES
    rowpos = qi * tq + lax.broadcasted_iota(I32, (tq, tk), 0)
    colpos0 = lax.broadcasted_iota(I32, (tq, tk), 1)

    def score_chunk(c, _):
        off = pl.multiple_of(c * tk, tk)
        kc = k_ref[pl.ds(off, tk), :]
        acc = jnp.zeros((tq, tk), F32)
        for h in range(IDX_HEADS):
            qh = q_ref[:, h * IDX_DIM:(h + 1) * IDX_DIM]
            d = lax.dot_general(qh, kc, (((1,), (1,)), ((), ())), preferred_element_type=F32)
            acc = acc + jnp.maximum(d, 0.0) * w_ref[:, h:h + 1]
        bits = pltpu.bitcast(acc, I32)
        key = bits ^ ((bits >> 31) & 0x7FFFFFFF)
        key = jnp.where(colpos0 + off <= rowpos, key, INT_MIN)
        key_scr[:, pl.ds(off, tk)] = key
        return 0

    lax.fori_loop(0, nck, score_chunk, 0)

    def count(pred):
        def body(c, acc):
            off = pl.multiple_of(c * tk, tk)
            kk = key_scr[:, pl.ds(off, tk)]
            for t in range(nt):
                acc = acc + jnp.where(pred(kk[:, t * LANES:(t + 1) * LANES]), 1, 0)
            return acc
        part = lax.fori_loop(0, nck, body, jnp.zeros((tq, LANES), I32))
        return jnp.sum(part, axis=-1, keepdims=True)

    def bisect(i, t_u):
        cand_u = t_u | lax.shift_left(jnp.int32(1), 31 - i)
        cand = cand_u ^ INT_MIN
        cnt = count(lambda kk: kk >= cand)
        return jnp.where(cnt >= topk, cand_u, t_u)

    t_u = lax.fori_loop(0, 32, bisect, jnp.zeros((tq, 1), I32))
    thr = t_u ^ INT_MIN
    has_thr = thr > INT_MIN
    thr_sel = jnp.maximum(thr, INT_MIN + 1)
    n_ge = count(lambda kk: kk >= thr)
    tie = jnp.max(jnp.where(has_thr & (n_ge > topk), 1, 0)) > 0

    def write_rest():
        def body(c, _):
            off = pl.multiple_of(c * tk, tk)
            o_ref[:, pl.ds(off, tk)] = jnp.full((tq, tk), NEG, o_ref.dtype)
            return 0
        lax.fori_loop(nck, S // tk, body, 0)

    @pl.when(jnp.logical_not(tie))
    def _():
        def body(c, _):
            off = pl.multiple_of(c * tk, tk)
            kk = key_scr[:, pl.ds(off, tk)]
            o_ref[:, pl.ds(off, tk)] = jnp.where(kk >= thr_sel, 0.0, NEG).astype(o_ref.dtype)
            return 0
        lax.fori_loop(0, nck, body, 0)

    @pl.when(tie)
    def _():
        n_gt = count(lambda kk: kk > thr)
        room = jnp.where(has_thr, topk - n_gt, 0).astype(F32)
        r = lax.broadcasted_iota(I32, (tk, tk), 0)
        cc = lax.broadcasted_iota(I32, (tk, tk), 1)
        upper = (r <= cc).astype(BF16)

        def body(c, run):
            off = pl.multiple_of(c * tk, tk)
            kk = key_scr[:, pl.ds(off, tk)]
            eq = kk == thr
            eq_f = jnp.where(eq, 1.0, 0.0)
            rank = jnp.dot(eq_f.astype(BF16), upper, preferred_element_type=F32) + run
            sel = (kk > thr) | (eq & (rank <= room))
            o_ref[:, pl.ds(off, tk)] = jnp.where(sel, 0.0, NEG).astype(o_ref.dtype)
            return run + jnp.sum(eq_f, axis=-1, keepdims=True)
        lax.fori_loop(0, nck, body, jnp.zeros((tq, 1), F32))

    write_rest()


def _indexer(qq, kidx, widx, *, B, S, topk, q_blk):
    M = B * S
    tq = BLOCK
    tk = _pick_tile(S, 512)
    nq = S // tq
    vm = (2 * (_nbytes((tq, IDX_WIDTH), BF16) + _nbytes((S, LANES), BF16) + _nbytes((tq, LANES), F32)
               + _nbytes((tq, S), BF16)) + _nbytes((tq, S), I32) + 16 * _nbytes((tq, tk), F32))
    return pl.pallas_call(
        functools.partial(_indexer_body, tq=tq, tk=tk, S=S, topk=topk),
        grid=(B, nq),
        in_specs=[pl.BlockSpec((tq, IDX_WIDTH), lambda b, i: (b * nq + i, q_blk)),
                  pl.BlockSpec((S, IDX_DIM), lambda b, i: (b, 0)),
                  pl.BlockSpec((tq, IDX_HEADS), lambda b, i: (b * nq + i, 0))],
        out_specs=pl.BlockSpec((tq, S), lambda b, i: (b * nq + i, 0)),
        out_shape=jax.ShapeDtypeStruct((M, S), BF16),
        scratch_shapes=[pltpu.VMEM((tq, S), I32)],
        compiler_params=_cparams(2, vm),
        name="dsa_indexer_topk",
    )(qq, kidx, widx)


def _t5_bucket(rel):
    n = jnp.maximum(rel, 0)
    max_exact = T5_BUCKETS // 2
    nf = jnp.maximum(n, 1).astype(F32)
    large = max_exact + (jnp.log(nf / max_exact) / math.log(T5_MAX_DIST / max_exact)
                         * (T5_BUCKETS - max_exact)).astype(I32)
    large = jnp.minimum(large, T5_BUCKETS - 1)
    return jnp.where(n < max_exact, n, large)


BAND_HEADS_PER_STEP = 8


def _band_body(bucket_ref, t5_ref, o_ref):
    base = pl.program_id(0) * BAND_HEADS_PER_STEP
    bucket = bucket_ref[...]
    for i in range(BAND_HEADS_PER_STEP):
        out = jnp.zeros(bucket.shape, F32)
        for u in range(T5_BUCKETS):
            out = jnp.where(bucket == u, t5_ref[u, base + i], out)
        o_ref[i] = out


def _band_tables(t5_table):
    n_cols = t5_table.shape[1]
    qi = jnp.arange(BLOCK)[:, None]
    kj = jnp.arange(2 * BLOCK)[None, :]
    bucket = _t5_bucket(qi + BLOCK - kj).astype(I32)
    band = pl.pallas_call(
        _band_body,
        grid=(n_cols // BAND_HEADS_PER_STEP,),
        in_specs=[pl.BlockSpec((BLOCK, 2 * BLOCK), lambda h: (0, 0)),
                  pl.BlockSpec(memory_space=pltpu.SMEM)],
        out_specs=pl.BlockSpec((BAND_HEADS_PER_STEP, BLOCK, 2 * BLOCK), lambda h: (h, 0, 0)),
        out_shape=jax.ShapeDtypeStruct((n_cols, BLOCK, 2 * BLOCK), F32),
        compiler_params=_cparams(1, 4 * _nbytes((BAND_HEADS_PER_STEP, BLOCK, 2 * BLOCK), F32)),
        name="t5_band_bias",
    )(bucket, t5_table.astype(F32))
    in_window = (kj - qi >= 1) & (kj - qi <= BLOCK)
    keep = jnp.stack([in_window & (kj >= BLOCK), in_window])[:, None]
    swa = jnp.where(keep, band[None, :SWA_Q_HEADS] * LOG2E, NEG)
    far = t5_table.astype(F32)[T5_BUCKETS - 1, SWA_Q_HEADS:]
    dsa = (band[SWA_Q_HEADS:] - far[:, None, None]) * LOG2E
    return swa, dsa


def _flash_tile(S):
    return _pick_tile(S, 512)


def _even_layer(h, hn, w_in_t, b_f, sinks, w_out, layer, band_swa, B, S):
    tn = PROJ_TILE
    f_col = 3 * FOX_WIDTH
    n_main = 4 * FOX_WIDTH + 2 * SWA_WIDTH + 2 * SWA_KV_WIDTH
    assert f_col % tn == 0 and FOX_WIDTH % tn == 0 and SWA_WIDTH % tn == 0 and 2 * SWA_KV_WIDTH == tn
    blk = lambda cols: cols // tn
    qb_lo = blk(4 * FOX_WIDTH)
    kv_blk = blk(4 * FOX_WIDTH + SWA_WIDTH)
    last = blk(n_main) - 1

    def out_block(j):
        return jnp.where(j == kv_blk, last, jnp.where(j > kv_blk, j - 1, j))

    w_main = _Weight(w_in_t, layer, transposed=True, skip=FOX_HEADS, skip_from=blk(f_col),
                     scales=((0, blk(FOX_WIDTH), FOX_HEAD_DIM ** -0.5 * LOG2E),
                             (qb_lo, qb_lo + blk(SWA_WIDTH), SWA_HEAD_DIM ** -0.5 * LOG2E)))
    proj = _matmul([hn], [w_main], n_main, BF16, tn=tn, out_block=out_block, name="even_in_proj")
    f = _matmul([hn], [_Weight(w_in_t, layer, transposed=True, col0=f_col)], LANES, F32,
                tn=LANES, name="even_gate_proj")
    decay = _fox_cum(f, b_f, B, S)
    nh = FOX_HEADS
    ya = _flash(proj, proj, proj, proj, B=B, S=S, H=nh, q_blk=0, k_blk=nh, v_blk=2 * nh,
                g_blk=3 * nh, hb=4, T=_flash_tile(S), decay=decay)
    base = 4 * FOX_WIDTH
    yb = _swa(proj, sinks, band_swa, B=B, S=S,
              q_blk=base // SWA_WIDTH, g_blk=base // SWA_WIDTH + 1,
              k_blk=(base + 2 * SWA_WIDTH) // SWA_KV_WIDTH,
              v_blk=(base + 2 * SWA_WIDTH) // SWA_KV_WIDTH + 1)
    return _matmul([ya, yb], [_Weight(w_out, layer), _Weight(w_out, layer, row0=FOX_WIDTH)],
                   h.shape[1], F32, residual=h, tn=tn, name="even_out_proj")


def _odd_layer(h, hn, w_in_t, q_norm_g, kv_norm_g, w_uq, w_uq_idx, ln_g, ln_b, w_uk, w_uv, w_out,
               layer, band_dsa, B, S, topk):
    tn = PROJ_TILE
    n_lat = DSA_Q_LATENT + DSA_KV_LATENT
    n_small = n_lat + IDX_DIM + IDX_HEADS
    assert n_lat % tn == 0 and n_small - n_lat <= LANES
    win = functools.partial(_Weight, w_in_t, layer, transposed=True)
    lat = _matmul([hn], [win()], n_lat, F32, tn=tn, name="odd_latent_proj")
    tail = _matmul([hn], [win(col0=n_lat)], LANES, F32, tn=LANES, name="odd_index_proj")
    gc = _matmul([hn], [win(col0=n_small)], DSA_WIDTH, BF16, tn=tn, name="odd_gate_proj")
    cq, ckv, kidx, widx = _odd_post(lat, tail, q_norm_g, kv_norm_g, ln_g, ln_b)
    tq = 2 * tn
    w_q = _Weight(w_uq, layer, scales=((0, DSA_WIDTH // tq, DSA_QK_DIM ** -0.5 * LOG2E),))
    q = _matmul([cq], [w_q], DSA_WIDTH, BF16, tn=tq, name="odd_q_up")
    q_idx = _matmul([cq], [_Weight(w_uq_idx, layer)], IDX_WIDTH, BF16, tn=tq, name="odd_qidx_up")
    n_odd, L = w_uk.shape[:2]
    k = _matmul([ckv], [_Weight(w_uk.reshape(n_odd, L, DSA_WIDTH), layer)], DSA_WIDTH, BF16,
                tn=tq, name="odd_k_up")
    v = _matmul([ckv], [_Weight(w_uv.reshape(n_odd, L, DSA_WIDTH), layer)], DSA_WIDTH, BF16,
                tn=tq, name="odd_v_up")
    mask_bias = _indexer(q_idx, kidx, widx, B=B, S=S, topk=topk, q_blk=0)
    y = _flash(q, k, v, gc, B=B, S=S, H=DSA_HEADS, q_blk=0, k_blk=0, v_blk=0, g_blk=0, hb=2,
               T=_flash_tile(S), mask_bias=mask_bias, band=band_dsa)
    return _matmul([y], [_Weight(w_out, layer)], h.shape[1], F32, residual=h, tn=tn,
                   name="odd_out_proj")


def kernel(x, p, t5_table, norm_g, even_w_in, even_b_f, even_sinks, even_w_out, odd_w_in, odd_q_norm_g, odd_kv_norm_g, odd_w_uq, odd_w_uq_idx, odd_idx_ln_g, odd_idx_ln_b, odd_w_uk, odd_w_uv, odd_w_out, ple_w_proj, ple_norm_g, ple_w_gate, final_g):
    B, S, D = x.shape
    depth = norm_g.shape[0]
    M = B * S
    topk = min(TOPK_MAX, S // 4)
    band_swa, band_dsa = _band_tables(t5_table)
    h = x.reshape(M, D).astype(F32)
    p_all = p.reshape(depth, M, p.shape[-1]).astype(F32)
    even_w_in_t = jnp.swapaxes(even_w_in, 1, 2)
    odd_w_in_t = jnp.swapaxes(odd_w_in, 1, 2)
    for i in range(depth):
        j = i // 2
        hn = _rmsnorm(h, norm_g[i], BF16)
        if i % 2 == 0:
            h = _even_layer(h, hn, even_w_in_t, even_b_f[j], even_sinks[j], even_w_out, j,
                            band_swa, B, S)
        else:
            h = _odd_layer(h, hn, odd_w_in_t, odd_q_norm_g[j], odd_kv_norm_g[j], odd_w_uq,
                           odd_w_uq_idx, odd_idx_ln_g[j], odd_idx_ln_b[j], odd_w_uk,
                           odd_w_uv, odd_w_out, j, band_dsa, B, S, topk)
        hn = _rmsnorm(h, ple_norm_g[i], BF16)
        h = _ple(hn, ple_w_gate, p_all, i, ple_w_proj, h)
    return _rmsnorm(h, final_g, F32).reshape(B, S, D)
```

```python
import functools
import math
from typing import NamedTuple

import jax
import jax.numpy as jnp
from jax import lax
from jax.experimental import pallas as pl
from jax.experimental.pallas import tpu as pltpu

F32 = jnp.float32
BF16 = jnp.bfloat16
I32 = jnp.int32

EPS = 1e-6
BLOCK = 128
PLE_DIM = 256
FOX_HEADS = 16
FOX_HEAD_DIM = 128
FOX_WIDTH = FOX_HEADS * FOX_HEAD_DIM
SWA_Q_HEADS = 32
SWA_KV_HEADS = 4
SWA_GROUP = SWA_Q_HEADS // SWA_KV_HEADS
SWA_HEAD_DIM = 64
SWA_WIDTH = SWA_Q_HEADS * SWA_HEAD_DIM
SWA_KV_WIDTH = SWA_KV_HEADS * SWA_HEAD_DIM
DSA_HEADS = 32
DSA_QK_DIM = 128
DSA_V_DIM = 128
DSA_WIDTH = DSA_HEADS * DSA_V_DIM
DSA_Q_LATENT = 1024
DSA_KV_LATENT = 512
IDX_HEADS = 32
IDX_DIM = 64
IDX_WIDTH = IDX_HEADS * IDX_DIM
TOPK_MAX = 256
T5_BUCKETS = 32
T5_MAX_DIST = 128

LANES = 128
SUBLANES = 8
V7X_VMEM_BYTES = 64 * 1024 * 1024
PROJ_TILE = 512
NEG = -1e30
INT_MIN = -2147483648
LOG2E = math.log2(math.e)


def _cparams(n_axes, vmem_bytes):
    limit = min(int(vmem_bytes * 1.25) + (4 << 20), V7X_VMEM_BYTES - (8 << 20))
    return pltpu.CompilerParams(dimension_semantics=("arbitrary",) * n_axes,
                                vmem_limit_bytes=limit)


def _pick_tile(n, cap):
    if n <= cap:
        return n
    best = None
    for d in range(LANES, cap + 1, LANES):
        if n % d == 0:
            best = d
    assert best is not None, (n, cap)
    return best


def _nbytes(shape, dtype):
    n = 1
    for s in shape:
        n *= s
    return n * jnp.dtype(dtype).itemsize


def _rmsnorm_body(x_ref, g_ref, o_ref):
    x = x_ref[...]
    ms = jnp.mean(x * x, axis=-1, keepdims=True)
    o_ref[...] = (x * lax.rsqrt(ms + EPS) * g_ref[...]).astype(o_ref.dtype)


def _rmsnorm(x, g, out_dtype):
    M, D = x.shape
    tm = _pick_tile(M, 256)
    vm = 2 * (_nbytes((tm, D), F32) + _nbytes((tm, D), out_dtype)) + 3 * _nbytes((tm, D), F32)
    return pl.pallas_call(
        _rmsnorm_body,
        grid=(M // tm,),
        in_specs=[pl.BlockSpec((tm, D), lambda i: (i, 0)),
                  pl.BlockSpec((1, D), lambda i: (0, 0))],
        out_specs=pl.BlockSpec((tm, D), lambda i: (i, 0)),
        out_shape=jax.ShapeDtypeStruct((M, D), out_dtype),
        compiler_params=_cparams(1, vm),
        name="rmsnorm",
    )(x, g.reshape(1, D).astype(F32))


class _Weight(NamedTuple):
    w: jax.Array
    layer: int
    row0: int = 0
    col0: int = 0
    transposed: bool = False
    skip: int = 0
    skip_from: int = 0
    scales: tuple = ()


def _matmul_body(*refs, wts, has_res):
    n = len(wts)
    x_refs, w_refs = refs[:n], refs[n:2 * n]
    r_ref = refs[2 * n] if has_res else None
    o_ref = refs[2 * n + (1 if has_res else 0)]
    w_scrs = refs[2 * n + (2 if has_res else 1):]
    j = pl.program_id(0)

    @pl.when(pl.program_id(1) == 0)
    def _():
        for wt, w_ref, w_scr in zip(wts, w_refs, w_scrs):
            w = w_ref[0] if wt.transposed else w_ref[...]
            scale = None
            for lo, hi, val in wt.scales:
                scale = jnp.where((j >= lo) & (j < hi), val, 1.0 if scale is None else scale)
            if scale is not None:
                w = w * scale
            w_scr[...] = w.astype(BF16)

    acc = None
    for wt, x_ref, w_scr in zip(wts, x_refs, w_scrs):
        contract = (((1,), (1,)), ((), ())) if wt.transposed else (((1,), (0,)), ((), ()))
        d = lax.dot_general(x_ref[...], w_scr[...], contract, preferred_element_type=F32)
        acc = d if acc is None else acc + d
    if has_res:
        acc = acc + r_ref[...]
    o_ref[...] = acc.astype(o_ref.dtype)


def _weight_spec(wt, K, tn):
    layer = wt.layer
    if wt.transposed:
        assert wt.row0 == 0 and wt.col0 % SUBLANES == 0 and wt.skip % SUBLANES == 0

        def start(j):
            skip = jnp.where(j >= wt.skip_from, wt.skip, 0) if wt.skip else 0
            return pl.multiple_of(wt.col0 + j * tn + skip, SUBLANES)
        dims = (pl.Element(1), pl.Element(tn), pl.Element(K))
        return pl.BlockSpec(dims, lambda j, i: (layer, start(j), 0)), (tn, K)
    assert wt.row0 % K == 0 and wt.col0 % tn == 0 and tn % LANES == 0 and wt.skip == 0
    rb, cb = wt.row0 // K, wt.col0 // tn
    return pl.BlockSpec((None, K, tn), lambda j, i: (layer, rb, cb + j)), (K, tn)


def _matmul(xs, wts, n_out, out_dtype, *, residual=None, tm_cap=1024, tn=512, out_block=None,
            name="matmul"):
    M = xs[0].shape[0]
    tm = _pick_tile(M, tm_cap)
    tn = min(tn, n_out)
    assert n_out % tn == 0
    out_block = out_block or (lambda j: j)
    in_specs, args, vm, scratch = [], [], 0, []
    for x in xs:
        K = x.shape[1]
        in_specs.append(pl.BlockSpec((tm, K), lambda j, i: (i, 0)))
        args.append(x)
        vm += 2 * _nbytes((tm, K), x.dtype)
    for x, wt in zip(xs, wts):
        spec, tile = _weight_spec(wt, x.shape[1], tn)
        in_specs.append(spec)
        args.append(wt.w)
        vm += 2 * _nbytes(tile, F32) + _nbytes(tile, BF16)
        scratch.append(pltpu.VMEM(tile, BF16))
    if residual is not None:
        in_specs.append(pl.BlockSpec((tm, tn), lambda j, i: (i, out_block(j))))
        args.append(residual)
        vm += 2 * _nbytes((tm, tn), F32)
    vm += 2 * _nbytes((tm, tn), out_dtype) + 2 * _nbytes((tm, tn), F32)
    return pl.pallas_call(
        functools.partial(_matmul_body, wts=tuple(wt._replace(w=None) for wt in wts),
                          has_res=residual is not None),
        grid=(n_out // tn, M // tm),
        in_specs=in_specs,
        out_specs=pl.BlockSpec((tm, tn), lambda j, i: (i, out_block(j))),
        out_shape=jax.ShapeDtypeStruct((M, n_out), out_dtype),
        scratch_shapes=scratch,
        compiler_params=_cparams(2, vm),
        name=name,
    )(*args)


def _ple_body(hn_ref, wg_ref, p_ref, wp_ref, h_ref, o_ref, wg_scr, wp_scr):
    @pl.when(pl.program_id(1) == 0)
    def _():
        wg_scr[...] = wg_ref[...].astype(BF16)
        wp_scr[...] = wp_ref[...].astype(BF16)

    z = jnp.dot(hn_ref[...], wg_scr[...], preferred_element_type=F32)
    gate = 1.0 / (1.0 + jnp.exp(-z))
    pe = jnp.dot(p_ref[...].astype(BF16), wp_scr[...], preferred_element_type=F32)
    o_ref[...] = h_ref[...] + gate * pe


def _ple(hn, wg_all, p_all, layer, wp_all, h):
    M, D = h.shape
    P = p_all.shape[-1]
    tm = _pick_tile(M, 1024)
    tn = _pick_tile(D, PROJ_TILE)
    vm = (2 * (_nbytes((tm, D), BF16) + _nbytes((D, tn), F32) + _nbytes((tm, P), F32)
               + _nbytes((P, tn), F32) + 2 * _nbytes((tm, tn), F32))
          + _nbytes((D + P, tn), BF16) + 3 * _nbytes((tm, tn), F32))
    return pl.pallas_call(
        _ple_body,
        grid=(D // tn, M // tm),
        in_specs=[pl.BlockSpec((tm, D), lambda j, i: (i, 0)),
                  pl.BlockSpec((None, D, tn), lambda j, i: (layer, 0, j)),
                  pl.BlockSpec((None, tm, P), lambda j, i: (layer, i, 0)),
                  pl.BlockSpec((None, P, tn), lambda j, i: (layer, 0, j)),
                  pl.BlockSpec((tm, tn), lambda j, i: (i, j))],
        out_specs=pl.BlockSpec((tm, tn), lambda j, i: (i, j)),
        out_shape=jax.ShapeDtypeStruct((M, D), F32),
        scratch_shapes=[pltpu.VMEM((D, tn), BF16), pltpu.VMEM((P, tn), BF16)],
        compiler_params=_cparams(2, vm),
        name="ple_gate",
    )(hn, wg_all, p_all, wp_all, h)


def _cum_body(f_ref, b_ref, o_ref, scr, *, S):
    x = f_ref[...] + b_ref[...]
    neg_log_f = jnp.maximum(-x, 0.0) + jnp.log1p(jnp.exp(-jnp.abs(x)))
    r = lax.broadcasted_iota(I32, (BLOCK, BLOCK), 0)
    c = lax.broadcasted_iota(I32, (BLOCK, BLOCK), 1)
    tri = (c <= r).astype(BF16)
    run = jnp.zeros((1, LANES), F32)
    for i in range(S // BLOCK):
        blk = neg_log_f[i * BLOCK:(i + 1) * BLOCK, :]
        hi = blk.astype(BF16)
        r1 = blk - hi.astype(F32)
        mid = r1.astype(BF16)
        lo = (r1 - mid.astype(F32)).astype(BF16)
        cs = (jnp.dot(tri, hi, preferred_element_type=F32)
              + jnp.dot(tri, mid, preferred_element_type=F32)
              + jnp.dot(tri, lo, preferred_element_type=F32)) + run
        scr[i * BLOCK:(i + 1) * BLOCK, :] = cs
        run = cs[BLOCK - 1:BLOCK, :]
    o_ref[0] = (scr[...] * LOG2E).T[:FOX_HEADS, :]


def _fox_cum(f, b_f, B, S):
    b_pad = jnp.zeros((1, LANES), F32).at[0, :FOX_HEADS].set(b_f.astype(F32))
    vm = 6 * _nbytes((S, LANES), F32)
    return pl.pallas_call(
        functools.partial(_cum_body, S=S),
        grid=(B,),
        in_specs=[pl.BlockSpec((S, LANES), lambda b: (b, 0)),
                  pl.BlockSpec((1, LANES), lambda b: (0, 0))],
        out_specs=pl.BlockSpec((1, FOX_HEADS, S), lambda b: (b, 0, 0)),
        out_shape=jax.ShapeDtypeStruct((B, FOX_HEADS, S), F32),
        scratch_shapes=[pltpu.VMEM((S, LANES), F32)],
        compiler_params=_cparams(1, vm),
        name="fox_decay_cumsum",
    )(f, b_pad)


def _flash_body(*refs, T, hb, fox):
    if fox:
        q_ref, k_ref, v_ref, g_ref, c_ref, o_ref = refs[:6]
    else:
        q_ref, k_ref, v_ref, g_ref, mb_ref, d_ref, o_ref, bias_scr, mask_scr = refs[:9]
    s0_scr, s1_scr, m_scr, l_scr, acc_scr = refs[-5:]
    qi = pl.program_id(1)
    hd = LANES
    head_cols = [slice(hh * hd, (hh + 1) * hd) for hh in range(hb)]

    if not fox:
        hg = pl.program_id(2)

        @pl.when((pl.program_id(0) == 0) & (qi == 0) & (hg == 0))
        def _():
            bias_scr[...] = jnp.zeros(bias_scr.shape, F32)

        @pl.when(hg == 0)
        def _():
            def widen(c, _):
                off = pl.multiple_of(c * T, T)
                mask_scr[:, pl.ds(off, T)] = mb_ref[:, pl.ds(off, T)].astype(F32)
                return 0
            lax.fori_loop(0, qi + 1, widen, 0)

        for hh in range(hb):
            for a in range(T // BLOCK):
                lo = T + (a - 1) * BLOCK
                bias_scr[hh, a * BLOCK:(a + 1) * BLOCK, lo:lo + 2 * BLOCK] = d_ref[hh]

    m_scr[...] = jnp.full(m_scr.shape, NEG, F32)
    l_scr[...] = jnp.zeros(l_scr.shape, F32)
    acc_scr[...] = jnp.zeros(acc_scr.shape, F32)

    def scores(c, buf, kind):
        off = pl.multiple_of(c * T, T)
        for hh, cols in enumerate(head_cols):
            s = lax.dot_general(q_ref[:, cols], k_ref[pl.ds(off, T), cols],
                                (((1,), (1,)), ((), ())), preferred_element_type=F32)
            if fox:
                s = s + c_ref[0, hh:hh + 1, pl.ds(off, T)]
                if kind == "diag":
                    row = lax.broadcasted_iota(I32, (T, T), 0)
                    col = lax.broadcasted_iota(I32, (T, T), 1)
                    s = jnp.where(row >= col, s, NEG)
            else:
                s = s + mask_scr[:, pl.ds(off, T)]
                if kind == "prev":
                    s = s + bias_scr[hh, :, 0:T]
                elif kind == "diag":
                    s = s + bias_scr[hh, :, T:2 * T]
            buf[hh] = s

    def absorb(c, buf):
        off = pl.multiple_of(c * T, T)
        for hh, cols in enumerate(head_cols):
            m_old = m_scr[hh]
            m_new = jnp.maximum(m_old, jnp.max(buf[hh], axis=-1, keepdims=True))
            m_scr[hh] = m_new
            alpha = jnp.exp2(m_old - m_new)
            p = jnp.exp2(buf[hh] - m_new)
            l_scr[hh] = alpha * l_scr[hh] + jnp.sum(p, axis=-1, keepdims=True)
            v = v_ref[pl.ds(off, T), cols]
            acc_scr[hh] = alpha * acc_scr[hh] + jnp.dot(p.astype(BF16), v,
                                                        preferred_element_type=F32)

    bufs = (s0_scr, s1_scr)
    n_plain = qi if fox else jnp.maximum(qi - 1, 0)
    n_pairs = jnp.maximum(n_plain - 1, 0) // 2
    pl.when(n_plain > 0)(lambda: scores(0, s0_scr, "plain"))

    def pair(u, _):
        c = 2 * u
        scores(c + 1, s1_scr, "plain")
        absorb(c, s0_scr)
        scores(c + 2, s0_scr, "plain")
        absorb(c + 1, s1_scr)
        return 0

    lax.fori_loop(0, n_pairs, pair, 0)
    c0 = 2 * n_pairs
    left = n_plain - c0

    def drain(kinds, primed):
        if not primed:
            scores(c0, bufs[0], kinds[0])
        for t, kind in enumerate(kinds):
            if t + 1 < len(kinds):
                scores(c0 + t + 1, bufs[(t + 1) % 2], kinds[t + 1])
            absorb(c0 + t, bufs[t % 2])

    special = ["diag"] if fox else ["prev", "diag"]
    pl.when(qi == 0)(lambda: drain(["diag"], False))
    if not fox:
        pl.when(qi == 1)(lambda: drain(["prev", "diag"], False))
    pl.when(left == 1)(lambda: drain(["plain"] + special, True))
    pl.when(left == 2)(lambda: drain(["plain", "plain"] + special, True))

    for hh, cols in enumerate(head_cols):
        g = g_ref[:, cols].astype(F32)
        o_ref[:, cols] = (acc_scr[hh] / l_scr[hh] * (g / (1.0 + jnp.exp(-g)))).astype(o_ref.dtype)


def _flash(q_arr, k_arr, v_arr, g_arr, *, B, S, H, q_blk, k_blk, v_blk, g_blk, hb, T,
           decay=None, mask_bias=None, band=None):
    fox = decay is not None
    M = B * S
    nq = S // T
    w = hb * LANES
    qb, kb, vb, gb = q_blk // hb, k_blk // hb, v_blk // hb, g_blk // hb
    in_specs = [pl.BlockSpec((T, w), lambda b, i, h: (b * nq + i, qb + h)),
                pl.BlockSpec((S, w), lambda b, i, h: (b, kb + h)),
                pl.BlockSpec((S, w), lambda b, i, h: (b, vb + h)),
                pl.BlockSpec((T, w), lambda b, i, h: (b * nq + i, gb + h))]
    args = [q_arr, k_arr, v_arr, g_arr]
    vm = (2 * (3 * _nbytes((T, w), BF16) + 2 * _nbytes((S, w), BF16))
          + 6 * hb * _nbytes((T, T), F32))
    scratch = []
    if fox:
        in_specs.append(pl.BlockSpec((1, hb, S), lambda b, i, h: (b * (H // hb) + h, 0, 0)))
        args.append(decay.reshape(B * H // hb, hb, S))
        vm += 2 * _nbytes((8, S), F32)
    else:
        in_specs.append(pl.BlockSpec((T, S), lambda b, i, h: (b * nq + i, 0)))
        in_specs.append(pl.BlockSpec((hb, BLOCK, 2 * BLOCK), lambda b, i, h: (h, 0, 0)))
        args += [mask_bias, band]
        scratch = [pltpu.VMEM((hb, T, 2 * T), F32), pltpu.VMEM((T, S), F32)]
        vm += (2 * _nbytes((T, S), BF16) + _nbytes((hb, T, 2 * T), F32) + _nbytes((T, S), F32)
               + 2 * _nbytes((hb, BLOCK, 2 * BLOCK), F32))
    scratch += [pltpu.VMEM((hb, T, T), F32), pltpu.VMEM((hb, T, T), F32),
                pltpu.VMEM((hb, T, 1), F32), pltpu.VMEM((hb, T, 1), F32),
                pltpu.VMEM((hb, T, LANES), F32)]
    vm += hb * (2 * _nbytes((T, T), F32) + 3 * _nbytes((T, LANES), F32))
    return pl.pallas_call(
        functools.partial(_flash_body, T=T, hb=hb, fox=fox),
        grid=(B, nq, H // hb),
        in_specs=in_specs,
        out_specs=pl.BlockSpec((T, w), lambda b, i, h: (b * nq + i, h)),
        out_shape=jax.ShapeDtypeStruct((M, H * LANES), BF16),
        scratch_shapes=scratch,
        compiler_params=_cparams(3, vm),
        name="fox_attention" if fox else "dsa_attention",
    )(*args)


def _swa_body(sink_ref, q_ref, kp_ref, kc_ref, vp_ref, vc_ref, g_ref, bias_ref, o_ref):
    G, dh = SWA_GROUP, SWA_HEAD_DIM
    rows = G * BLOCK
    outs = []
    for kvh in range(SWA_KV_HEADS):
        kcols = slice(kvh * dh, (kvh + 1) * dh)
        k = jnp.concatenate([kp_ref[:, kcols], kc_ref[:, kcols]], axis=0)
        v = jnp.concatenate([vp_ref[:, kcols], vc_ref[:, kcols]], axis=0)
        qs = jnp.concatenate(
            [q_ref[:, (kvh * G + i) * dh:(kvh * G + i + 1) * dh] for i in range(G)], axis=0)
        s = lax.dot_general(qs, k, (((1,), (1,)), ((), ())), preferred_element_type=F32)
        s = s + bias_ref[kvh * G:(kvh + 1) * G].reshape(rows, 2 * BLOCK)
        sink = sink_ref[kvh * rows:(kvh + 1) * rows, :]
        m = jnp.maximum(jnp.max(s, axis=-1, keepdims=True), sink)
        p = jnp.exp2(s - jnp.concatenate([m, m], axis=1))
        den = jnp.sum(p, axis=-1, keepdims=True) + jnp.exp2(sink - m)
        o = jnp.dot(p.astype(BF16), v, preferred_element_type=F32) / den[:, :dh]
        outs += [o[i * BLOCK:(i + 1) * BLOCK, :] for i in range(G)]
    o_all = jnp.concatenate(outs, axis=1)
    g = g_ref[...].astype(F32)
    o_ref[...] = (o_all * (g / (1.0 + jnp.exp(-g)))).astype(o_ref.dtype)


def _swa(proj, sinks, band, *, B, S, q_blk, g_blk, k_blk, v_blk):
    M = B * S
    nb = S // BLOCK
    W, KW = SWA_WIDTH, SWA_KV_WIDTH

    def prev(b, n):
        return b * nb + jnp.maximum(n - 1, 0)

    sink_col = jnp.broadcast_to(jnp.repeat(sinks.astype(F32) * LOG2E, BLOCK)[:, None],
                                (SWA_Q_HEADS * BLOCK, LANES))
    vm = (2 * (3 * _nbytes((BLOCK, W), BF16) + 4 * _nbytes((BLOCK, KW), BF16))
          + 2 * _nbytes((SWA_Q_HEADS * BLOCK, LANES), F32)
          + 2 * _nbytes(band.shape[1:], F32) + 16 * _nbytes((SWA_GROUP * BLOCK, 2 * BLOCK), F32))
    return pl.pallas_call(
        _swa_body,
        grid=(B, nb),
        in_specs=[pl.BlockSpec((SWA_Q_HEADS * BLOCK, LANES), lambda b, n: (0, 0)),
                  pl.BlockSpec((BLOCK, W), lambda b, n: (b * nb + n, q_blk)),
                  pl.BlockSpec((BLOCK, KW), lambda b, n: (prev(b, n), k_blk)),
                  pl.BlockSpec((BLOCK, KW), lambda b, n: (b * nb + n, k_blk)),
                  pl.BlockSpec((BLOCK, KW), lambda b, n: (prev(b, n), v_blk)),
                  pl.BlockSpec((BLOCK, KW), lambda b, n: (b * nb + n, v_blk)),
                  pl.BlockSpec((BLOCK, W), lambda b, n: (b * nb + n, g_blk)),
                  pl.BlockSpec((None,) + band.shape[1:],
                               lambda b, n: (jnp.minimum(n, 1), 0, 0, 0))],
        out_specs=pl.BlockSpec((BLOCK, W), lambda b, n: (b * nb + n, 0)),
        out_shape=jax.ShapeDtypeStruct((M, W), BF16),
        compiler_params=_cparams(2, vm),
        name="swa_sink_attention",
    )(sink_col, proj, proj, proj, proj, proj, proj, band)


def _odd_post_body(x_ref, t_ref, qg_ref, kvg_ref, lng_ref, lnb_ref, cq_ref, ckv_ref, kidx_ref,
                   w_ref):
    cq = x_ref[:, 0:DSA_Q_LATENT]
    cq_ref[...] = (cq * lax.rsqrt(jnp.mean(cq * cq, axis=-1, keepdims=True) + EPS)
                   * qg_ref[...]).astype(cq_ref.dtype)
    ckv = x_ref[:, DSA_Q_LATENT:DSA_Q_LATENT + DSA_KV_LATENT]
    ckv_ref[...] = (ckv * lax.rsqrt(jnp.mean(ckv * ckv, axis=-1, keepdims=True) + EPS)
                    * kvg_ref[...]).astype(ckv_ref.dtype)
    tail = t_ref[...]
    ki = tail[:, 0:IDX_DIM]
    mu = jnp.mean(ki, axis=-1, keepdims=True)
    xc = ki - mu
    var = jnp.mean(xc * xc, axis=-1, keepdims=True)
    kidx_ref[...] = (xc * lax.rsqrt(var + EPS) * lng_ref[...] + lnb_ref[...]).astype(kidx_ref.dtype)
    w_ref[...] = tail[:, IDX_DIM:IDX_DIM + IDX_HEADS] * (IDX_HEADS ** -0.5 * IDX_DIM ** -0.5)


def _odd_post(lat, tail, q_norm_g, kv_norm_g, ln_g, ln_b):
    M, n_lat = lat.shape
    tm = _pick_tile(M, 512)
    vm = 8 * _nbytes((tm, n_lat + LANES), F32)
    row = lambda i: (i, 0)
    fixed = lambda i: (0, 0)
    return pl.pallas_call(
        _odd_post_body,
        grid=(M // tm,),
        in_specs=[pl.BlockSpec((tm, n_lat), row),
                  pl.BlockSpec((tm, LANES), row),
                  pl.BlockSpec((1, DSA_Q_LATENT), fixed),
                  pl.BlockSpec((1, DSA_KV_LATENT), fixed),
                  pl.BlockSpec((1, IDX_DIM), fixed),
                  pl.BlockSpec((1, IDX_DIM), fixed)],
        out_specs=[pl.BlockSpec((tm, DSA_Q_LATENT), row),
                   pl.BlockSpec((tm, DSA_KV_LATENT), row),
                   pl.BlockSpec((tm, IDX_DIM), row),
                   pl.BlockSpec((tm, IDX_HEADS), row)],
        out_shape=[jax.ShapeDtypeStruct((M, DSA_Q_LATENT), BF16),
                   jax.ShapeDtypeStruct((M, DSA_KV_LATENT), BF16),
                   jax.ShapeDtypeStruct((M, IDX_DIM), BF16),
                   jax.ShapeDtypeStruct((M, IDX_HEADS), F32)],
        compiler_params=_cparams(1, vm),
        name="dsa_latent_norms",
    )(lat, tail, q_norm_g.reshape(1, -1).astype(F32), kv_norm_g.reshape(1, -1).astype(F32),
      ln_g.reshape(1, -1).astype(F32), ln_b.reshape(1, -1).astype(F32))


def _indexer_body(q_ref, k_ref, w_ref, o_ref, key_scr, *, tq, tk, S, topk):
    qi = pl.program_id(1)
    nck = (qi * tq) // tk + 1
    nt = tk // LANES
    rowpos = qi * tq + lax.broadcasted_iota(I32, (tq, tk), 0)
    colpos0 = lax.broadcasted_iota(I32, (tq, tk), 1)

    def score_chunk(c, _):
        off = pl.multiple_of(c * tk, tk)
        kc = k_ref[pl.ds(off, tk), :]
        acc = jnp.zeros((tq, tk), F32)
        for h in range(IDX_HEADS):
            qh = q_ref[:, h * IDX_DIM:(h + 1) * IDX_DIM]
            d = lax.dot_general(qh, kc, (((1,), (1,)), ((), ())), preferred_element_type=F32)
            acc = acc + jnp.maximum(d, 0.0) * w_ref[:, h:h + 1]
        bits = pltpu.bitcast(acc, I32)
        key = bits ^ ((bits >> 31) & 0x7FFFFFFF)
        key = jnp.where(colpos0 + off <= rowpos, key, INT_MIN)
        key_scr[:, pl.ds(off, tk)] = key
        return 0

    lax.fori_loop(0, nck, score_chunk, 0)

    def count(pred):
        def body(c, acc):
            off = pl.multiple_of(c * tk, tk)
            kk = key_scr[:, pl.ds(off, tk)]
            for t in range(nt):
                acc = acc + jnp.where(pred(kk[:, t * LANES:(t + 1) * LANES]), 1, 0)
            return acc
        part = lax.fori_loop(0, nck, body, jnp.zeros((tq, LANES), I32))
        return jnp.sum(part, axis=-1, keepdims=True)

    def bisect(i, t_u):
        cand_u = t_u | lax.shift_left(jnp.int32(1), 31 - i)
        cand = cand_u ^ INT_MIN
        cnt = count(lambda kk: kk >= cand)
        return jnp.where(cnt >= topk, cand_u, t_u)

    t_u = lax.fori_loop(0, 32, bisect, jnp.zeros((tq, 1), I32))
    thr = t_u ^ INT_MIN
    has_thr = thr > INT_MIN
    thr_sel = jnp.maximum(thr, INT_MIN + 1)
    n_ge = count(lambda kk: kk >= thr)
    tie = jnp.max(jnp.where(has_thr & (n_ge > topk), 1, 0)) > 0

    def write_rest():
        def body(c, _):
            off = pl.multiple_of(c * tk, tk)
            o_ref[:, pl.ds(off, tk)] = jnp.full((tq, tk), NEG, o_ref.dtype)
            return 0
        lax.fori_loop(nck, S // tk, body, 0)

    @pl.when(jnp.logical_not(tie))
    def _():
        def body(c, _):
            off = pl.multiple_of(c * tk, tk)
            kk = key_scr[:, pl.ds(off, tk)]
            o_ref[:, pl.ds(off, tk)] = jnp.where(kk >= thr_sel, 0.0, NEG).astype(o_ref.dtype)
            return 0
        lax.fori_loop(0, nck, body, 0)

    @pl.when(tie)
    def _():
        n_gt = count(lambda kk: kk > thr)
        room = jnp.where(has_thr, topk - n_gt, 0).astype(F32)
        r = lax.broadcasted_iota(I32, (tk, tk), 0)
        cc = lax.broadcasted_iota(I32, (tk, tk), 1)
        upper = (r <= cc).astype(BF16)

        def body(c, run):
            off = pl.multiple_of(c * tk, tk)
            kk = key_scr[:, pl.ds(off, tk)]
            eq = kk == thr
            eq_f = jnp.where(eq, 1.0, 0.0)
            rank = jnp.dot(eq_f.astype(BF16), upper, preferred_element_type=F32) + run
            sel = (kk > thr) | (eq & (rank <= room))
            o_ref[:, pl.ds(off, tk)] = jnp.where(sel, 0.0, NEG).astype(o_ref.dtype)
            return run + jnp.sum(eq_f, axis=-1, keepdims=True)
        lax.fori_loop(0, nck, body, jnp.zeros((tq, 1), F32))

    write_rest()


def _indexer(qq, kidx, widx, *, B, S, topk, q_blk):
    M = B * S
    tq = BLOCK
    tk = _pick_tile(S, 512)
    nq = S // tq
    vm = (2 * (_nbytes((tq, IDX_WIDTH), BF16) + _nbytes((S, LANES), BF16) + _nbytes((tq, LANES), F32)
               + _nbytes((tq, S), BF16)) + _nbytes((tq, S), I32) + 16 * _nbytes((tq, tk), F32))
    return pl.pallas_call(
        functools.partial(_indexer_body, tq=tq, tk=tk, S=S, topk=topk),
        grid=(B, nq),
        in_specs=[pl.BlockSpec((tq, IDX_WIDTH), lambda b, i: (b * nq + i, q_blk)),
                  pl.BlockSpec((S, IDX_DIM), lambda b, i: (b, 0)),
                  pl.BlockSpec((tq, IDX_HEADS), lambda b, i: (b * nq + i, 0))],
        out_specs=pl.BlockSpec((tq, S), lambda b, i: (b * nq + i, 0)),
        out_shape=jax.ShapeDtypeStruct((M, S), BF16),
        scratch_shapes=[pltpu.VMEM((tq, S), I32)],
        compiler_params=_cparams(2, vm),
        name="dsa_indexer_topk",
    )(qq, kidx, widx)


def _t5_bucket(rel):
    n = jnp.maximum(rel, 0)
    max_exact = T5_BUCKETS // 2
    nf = jnp.maximum(n, 1).astype(F32)
    large = max_exact + (jnp.log(nf / max_exact) / math.log(T5_MAX_DIST / max_exact)
                         * (T5_BUCKETS - max_exact)).astype(I32)
    large = jnp.minimum(large, T5_BUCKETS - 1)
    return jnp.where(n < max_exact, n, large)


BAND_HEADS_PER_STEP = 8


def _band_body(bucket_ref, t5_ref, o_ref):
    base = pl.program_id(0) * BAND_HEADS_PER_STEP
    bucket = bucket_ref[...]
    for i in range(BAND_HEADS_PER_STEP):
        out = jnp.zeros(bucket.shape, F32)
        for u in range(T5_BUCKETS):
            out = jnp.where(bucket == u, t5_ref[u, base + i], out)
        o_ref[i] = out


def _band_tables(t5_table):
    n_cols = t5_table.shape[1]
    qi = jnp.arange(BLOCK)[:, None]
    kj = jnp.arange(2 * BLOCK)[None, :]
    bucket = _t5_bucket(qi + BLOCK - kj).astype(I32)
    band = pl.pallas_call(
        _band_body,
        grid=(n_cols // BAND_HEADS_PER_STEP,),
        in_specs=[pl.BlockSpec((BLOCK, 2 * BLOCK), lambda h: (0, 0)),
                  pl.BlockSpec(memory_space=pltpu.SMEM)],
        out_specs=pl.BlockSpec((BAND_HEADS_PER_STEP, BLOCK, 2 * BLOCK), lambda h: (h, 0, 0)),
        out_shape=jax.ShapeDtypeStruct((n_cols, BLOCK, 2 * BLOCK), F32),
        compiler_params=_cparams(1, 4 * _nbytes((BAND_HEADS_PER_STEP, BLOCK, 2 * BLOCK), F32)),
        name="t5_band_bias",
    )(bucket, t5_table.astype(F32))
    in_window = (kj - qi >= 1) & (kj - qi <= BLOCK)
    keep = jnp.stack([in_window & (kj >= BLOCK), in_window])[:, None]
    swa = jnp.where(keep, band[None, :SWA_Q_HEADS] * LOG2E, NEG)
    far = t5_table.astype(F32)[T5_BUCKETS - 1, SWA_Q_HEADS:]
    dsa = (band[SWA_Q_HEADS:] - far[:, None, None]) * LOG2E
    return swa, dsa


def _flash_tile(S):
    return _pick_tile(S, 512)


def _even_layer(h, hn, w_in_t, b_f, sinks, w_out, layer, band_swa, B, S):
    tn = PROJ_TILE
    f_col = 3 * FOX_WIDTH
    n_main = 4 * FOX_WIDTH + 2 * SWA_WIDTH + 2 * SWA_KV_WIDTH
    assert f_col % tn == 0 and FOX_WIDTH % tn == 0 and SWA_WIDTH % tn == 0 and 2 * SWA_KV_WIDTH == tn
    blk = lambda cols: cols // tn
    qb_lo = blk(4 * FOX_WIDTH)
    kv_blk = blk(4 * FOX_WIDTH + SWA_WIDTH)
    last = blk(n_main) - 1

    def out_block(j):
        return jnp.where(j == kv_blk, last, jnp.where(j > kv_blk, j - 1, j))

    w_main = _Weight(w_in_t, layer, transposed=True, skip=FOX_HEADS, skip_from=blk(f_col),
                     scales=((0, blk(FOX_WIDTH), FOX_HEAD_DIM ** -0.5 * LOG2E),
                             (qb_lo, qb_lo + blk(SWA_WIDTH), SWA_HEAD_DIM ** -0.5 * LOG2E)))
    proj = _matmul([hn], [w_main], n_main, BF16, tn=tn, out_block=out_block, name="even_in_proj")
    f = _matmul([hn], [_Weight(w_in_t, layer, transposed=True, col0=f_col)], LANES, F32,
                tn=LANES, name="even_gate_proj")
    decay = _fox_cum(f, b_f, B, S)
    nh = FOX_HEADS
    ya = _flash(proj, proj, proj, proj, B=B, S=S, H=nh, q_blk=0, k_blk=nh, v_blk=2 * nh,
                g_blk=3 * nh, hb=4, T=_flash_tile(S), decay=decay)
    base = 4 * FOX_WIDTH
    yb = _swa(proj, sinks, band_swa, B=B, S=S,
              q_blk=base // SWA_WIDTH, g_blk=base // SWA_WIDTH + 1,
              k_blk=(base + 2 * SWA_WIDTH) // SWA_KV_WIDTH,
              v_blk=(base + 2 * SWA_WIDTH) // SWA_KV_WIDTH + 1)
    return _matmul([ya, yb], [_Weight(w_out, layer), _Weight(w_out, layer, row0=FOX_WIDTH)],
                   h.shape[1], F32, residual=h, tn=tn, name="even_out_proj")


def _odd_layer(h, hn, w_in_t, q_norm_g, kv_norm_g, w_uq, w_uq_idx, ln_g, ln_b, w_uk, w_uv, w_out,
               layer, band_dsa, B, S, topk):
    tn = PROJ_TILE
    n_lat = DSA_Q_LATENT + DSA_KV_LATENT
    n_small = n_lat + IDX_DIM + IDX_HEADS
    assert n_lat % tn == 0 and n_small - n_lat <= LANES
    win = functools.partial(_Weight, w_in_t, layer, transposed=True)
    lat = _matmul([hn], [win()], n_lat, F32, tn=tn, name="odd_latent_proj")
    tail = _matmul([hn], [win(col0=n_lat)], LANES, F32, tn=LANES, name="odd_index_proj")
    gc = _matmul([hn], [win(col0=n_small)], DSA_WIDTH, BF16, tn=tn, name="odd_gate_proj")
    cq, ckv, kidx, widx = _odd_post(lat, tail, q_norm_g, kv_norm_g, ln_g, ln_b)
    tq = 2 * tn
    w_q = _Weight(w_uq, layer, scales=((0, DSA_WIDTH // tq, DSA_QK_DIM ** -0.5 * LOG2E),))
    q = _matmul([cq], [w_q], DSA_WIDTH, BF16, tn=tq, name="odd_q_up")
    q_idx = _matmul([cq], [_Weight(w_uq_idx, layer)], IDX_WIDTH, BF16, tn=tq, name="odd_qidx_up")
    n_odd, L = w_uk.shape[:2]
    k = _matmul([ckv], [_Weight(w_uk.reshape(n_odd, L, DSA_WIDTH), layer)], DSA_WIDTH, BF16,
                tn=tq, name="odd_k_up")
    v = _matmul([ckv], [_Weight(w_uv.reshape(n_odd, L, DSA_WIDTH), layer)], DSA_WIDTH, BF16,
                tn=tq, name="odd_v_up")
    mask_bias = _indexer(q_idx, kidx, widx, B=B, S=S, topk=topk, q_blk=0)
    y = _flash(q, k, v, gc, B=B, S=S, H=DSA_HEADS, q_blk=0, k_blk=0, v_blk=0, g_blk=0, hb=2,
               T=_flash_tile(S), mask_bias=mask_bias, band=band_dsa)
    return _matmul([y], [_Weight(w_out, layer)], h.shape[1], F32, residual=h, tn=tn,
                   name="odd_out_proj")


def kernel(x, p, t5_table, norm_g, even_w_in, even_b_f, even_sinks, even_w_out, odd_w_in, odd_q_norm_g, odd_kv_norm_g, odd_w_uq, odd_w_uq_idx, odd_idx_ln_g, odd_idx_ln_b, odd_w_uk, odd_w_uv, odd_w_out, ple_w_proj, ple_norm_g, ple_w_gate, final_g):
    B, S, D = x.shape
    depth = norm_g.shape[0]
    M = B * S
    topk = min(TOPK_MAX, S // 4)
    band_swa, band_dsa = _band_tables(t5_table)
    h = x.reshape(M, D).astype(F32)
    p_all = p.reshape(depth, M, p.shape[-1]).astype(F32)
    even_w_in_t = jnp.swapaxes(even_w_in, 1, 2)
    odd_w_in_t = jnp.swapaxes(odd_w_in, 1, 2)
    for i in range(depth):
        j = i // 2
        hn = _rmsnorm(h, norm_g[i], BF16)
        if i % 2 == 0:
            h = _even_layer(h, hn, even_w_in_t, even_b_f[j], even_sinks[j], even_w_out, j,
                            band_swa, B, S)
        else:
            h = _odd_layer(h, hn, odd_w_in_t, odd_q_norm_g[j], odd_kv_norm_g[j], odd_w_uq,
                           odd_w_uq_idx, odd_idx_ln_g[j], odd_idx_ln_b[j], odd_w_uk,
                           odd_w_uv, odd_w_out, j, band_dsa, B, S, topk)
        hn = _rmsnorm(h, ple_norm_g[i], BF16)
        h = _ple(hn, ple_w_gate, p_all, i, ple_w_proj, h)
    return _rmsnorm(h, final_g, F32).reshape(B, S, D)
```
